```python
import math
import jax
import jax.numpy as jnp
from jax import lax
import numpy as np

D_MODEL = 1024
BATCH = 4
SEQ = 8192
DEPTH = 4

BRANCH_WIDTH = D_MODEL // 2
HEAD_DIM = 128
N_HEADS = BRANCH_WIDTH // HEAD_DIM
N_BRANCHES = 3
CONV_WIDTH = 4
CHUNK = 64
SB_BLOCK = 128
N_META = 16
FRONT = SB_BLOCK
PAD_FRONT = FRONT - N_META
NORM_EPS = 1e-6
SPLIT_SIZES = (BRANCH_WIDTH, BRANCH_WIDTH, BRANCH_WIDTH, BRANCH_WIDTH,
               3 * BRANCH_WIDTH, BRANCH_WIDTH, N_HEADS, N_HEADS,
               BRANCH_WIDTH, BRANCH_WIDTH, BRANCH_WIDTH, BRANCH_WIDTH,
               N_BRANCHES * D_MODEL)
N_IN = 12 * BRANCH_WIDTH + 2 * N_HEADS + N_BRANCHES * D_MODEL

kernel_name = 'meta_sb_gdn_hgrn2_gated_hybrid'


def rms_norm(x, w):
    xf = x.astype(jnp.float32)
    y = xf * lax.rsqrt(jnp.mean(xf * xf, axis=-1, keepdims=True) + NORM_EPS)
    return (y * w.astype(jnp.float32)).astype(x.dtype)


def l2_norm(x):
    return x * lax.rsqrt(jnp.sum(x * x, axis=-1, keepdims=True) + NORM_EPS)


def causal_conv(x, w):
    t_len = x.shape[1]
    xp = jnp.pad(x, ((0, 0), (CONV_WIDTH - 1, 0), (0, 0)))
    y = xp[:, 0:t_len] * w[0]
    for i in range(1, CONV_WIDTH):
        y = y + xp[:, i:i + t_len] * w[i]
    return y


def stick_breaking_attention(q, k, v, key_valid):
    t_len, d = q.shape[2], q.shape[3]
    scale = d ** -0.5
    outs = []
    for blk in range(t_len // SB_BLOCK):
        q0 = blk * SB_BLOCK
        q1 = q0 + SB_BLOCK
        z = jnp.einsum('bhqd,bhkd->bhqk', q[:, :, q0:q1], k[:, :, :q1]) * scale
        t_idx = jnp.arange(q0, q1)[:, None]
        s_idx = jnp.arange(q1)[None, :]
        mask = (s_idx < t_idx) & key_valid[None, :q1]
        log_keep = jnp.where(mask, jax.nn.log_sigmoid(-z), 0.0)
        log_passed = lax.cumsum(log_keep, axis=3, reverse=True) - log_keep
        attn = jnp.where(mask, jnp.exp(jax.nn.log_sigmoid(z) + log_passed), 0.0)
        outs.append(jnp.einsum('bhqk,bhkd->bhqd', attn, v[:, :, :q1]))
    return jnp.concatenate(outs, axis=2)


def to_chunks(a):
    bsz, t_len, h = a.shape[0], a.shape[1], a.shape[2]
    a = a.reshape((bsz, t_len // CHUNK, CHUNK, h) + a.shape[3:])
    return jnp.moveaxis(a, 3, 1)


def from_chunks(o):
    bsz, h, n, c, dv = o.shape
    return jnp.moveaxis(o, 1, 3).reshape(bsz, n * c, h, dv)


def gated_delta_rule_chunked(q, k, v, beta, g):
    bsz, _, h, dk = q.shape
    dv = v.shape[-1]
    q = to_chunks(q) * (dk ** -0.5)
    k = to_chunks(k)
    v = to_chunks(v)
    beta = to_chunks(beta)
    G = jnp.cumsum(to_chunks(g), axis=-1)
    causal = np.tril(np.ones((CHUNK, CHUNK), dtype=bool))
    strict = np.tril(np.ones((CHUNK, CHUNK), dtype=bool), k=-1)
    decay = jnp.exp(jnp.where(causal, G[..., :, None] - G[..., None, :], -jnp.inf))
    k_beta = k * beta[..., None]
    m = jnp.where(strict, jnp.einsum('bhncd,bhnsd->bhncs', k_beta, k) * decay, 0.0)
    t_mat = m + jnp.eye(CHUNK, dtype=jnp.float32)
    u = lax.linalg.triangular_solve(t_mat, v * beta[..., None], left_side=True, lower=True, unit_diagonal=True)
    w = lax.linalg.triangular_solve(t_mat, k_beta * jnp.exp(G)[..., None], left_side=True, lower=True, unit_diagonal=True)
    a_qk = jnp.einsum('bhncd,bhnsd->bhncs', q, k) * decay
    q_dec = q * jnp.exp(G)[..., None]
    k_dec = k * jnp.exp(G[..., -1:] - G)[..., None]
    g_last = jnp.exp(G[..., -1])
    xs = tuple(jnp.moveaxis(a, 2, 0) for a in (u, w, a_qk, q_dec, k_dec, g_last))

    def step(state, inp):
        u_c, w_c, aqk_c, qd_c, kd_c, gl_c = inp
        v_new = u_c - jnp.einsum('bhcd,bhde->bhce', w_c, state)
        o = jnp.einsum('bhcd,bhde->bhce', qd_c, state) + jnp.einsum('bhcs,bhse->bhce', aqk_c, v_new)
        state = state * gl_c[..., None, None] + jnp.einsum('bhcd,bhce->bhde', kd_c, v_new)
        return state, o

    s0 = jnp.zeros((bsz, h, dk, dv), jnp.float32)
    _, o = lax.scan(step, s0, xs)
    return from_chunks(jnp.moveaxis(o, 0, 2))


def hgrn2_chunked(q, k, v, g):
    bsz, _, h, dk = q.shape
    dv = v.shape[-1]
    G = jnp.cumsum(to_chunks(g), axis=3)
    xs = tuple(jnp.moveaxis(a, 2, 0) for a in (to_chunks(q), to_chunks(k), to_chunks(v), G))
    causal = np.tril(np.ones((CHUNK, CHUNK), dtype=bool))[:, :, None]

    def step(state, inp):
        q_c, k_c, v_c, g_c = inp
        g_end = g_c[:, :, -1:, :]
        o_inter = jnp.einsum('bhcd,bhde->bhce', q_c * jnp.exp(g_c), state)
        diff = g_c[:, :, :, None, :] - g_c[:, :, None, :, :]
        dec = jnp.exp(jnp.where(causal, diff, -jnp.inf))
        a = jnp.einsum('bhid,bhjd,bhijd->bhij', q_c, k_c, dec)
        o = o_inter + jnp.einsum('bhij,bhje->bhie', a, v_c)
        state = state * jnp.exp(g_end)[:, :, 0, :, None] + jnp.einsum('bhcd,bhce->bhde', k_c * jnp.exp(g_end - g_c), v_c)
        return state, o

    s0 = jnp.zeros((bsz, h, dk, dv), jnp.float32)
    _, o = lax.scan(step, s0, xs)
    return from_chunks(jnp.moveaxis(o, 0, 2))


def hybrid_layer(h, valid, norm_w, w_in, sb_qn, sb_kn, conv_w, a_log, dt_bias, gdn_on, lb, hg_on, w_branch, w_out):
    bsz, t_len, _ = h.shape
    f32 = jnp.float32
    xn = rms_norm(h, norm_w)
    proj = jnp.einsum('btd,dn->btn', xn, w_in)
    split_points = np.cumsum(SPLIT_SIZES)[:-1].tolist()
    (sb_q, sb_k, sb_v, sb_z, gd_qkv, gd_z, gd_b, gd_a,
     hg_q, hg_f, hg_i, hg_z, mix) = jnp.split(proj, split_points, axis=-1)
    vmask = valid[None, :, None].astype(f32)

    def heads(a):
        return a.reshape(bsz, t_len, N_HEADS, HEAD_DIM)

    q = jnp.transpose(rms_norm(heads(sb_q), sb_qn).astype(f32), (0, 2, 1, 3))
    k = jnp.transpose(rms_norm(heads(sb_k), sb_kn).astype(f32), (0, 2, 1, 3))
    v = jnp.transpose(heads(sb_v).astype(f32), (0, 2, 1, 3))
    o_sb = stick_breaking_attention(q, k, v, valid)
    o_sb = jnp.transpose(o_sb, (0, 2, 1, 3)).reshape(bsz, t_len, BRANCH_WIDTH) * jax.nn.silu(sb_z.astype(f32))

    qkv = jax.nn.silu(causal_conv(gd_qkv.astype(f32), conv_w.astype(f32)))
    gq, gk, gv = jnp.split(qkv, 3, axis=-1)
    beta = jax.nn.sigmoid(gd_b.astype(f32)) * vmask
    g = -jnp.exp(a_log.astype(f32)) * jax.nn.softplus(gd_a.astype(f32) + dt_bias.astype(f32))
    o_gd = gated_delta_rule_chunked(l2_norm(heads(gq)), l2_norm(heads(gk)), heads(gv), beta, g)
    o_gd = rms_norm(o_gd, gdn_on).reshape(bsz, t_len, BRANCH_WIDTH) * jax.nn.silu(gd_z.astype(f32))

    lbf = lb.astype(f32)
    f_pre = hg_f.astype(f32)
    forget = lbf + (1.0 - lbf) * jax.nn.sigmoid(f_pre)
    hk = (1.0 - lbf) * jax.nn.sigmoid(-f_pre)
    o_hg = hgrn2_chunked(heads(jax.nn.silu(hg_q.astype(f32))), heads(hk),
                         heads(hg_i.astype(f32) * vmask), heads(jnp.log(forget)))
    o_hg = rms_norm(o_hg, hg_on).reshape(bsz, t_len, BRANCH_WIDTH) * jax.nn.silu(hg_z.astype(f32))

    gates = jax.nn.sigmoid(mix.astype(f32)).reshape(bsz, t_len, N_BRANCHES, D_MODEL)
    y = (gates[:, :, 0] * jnp.einsum('btw,wd->btd', o_sb, w_branch[0])
         + gates[:, :, 1] * jnp.einsum('btw,wd->btd', o_gd, w_branch[1])
         + gates[:, :, 2] * jnp.einsum('btw,wd->btd', o_hg, w_branch[2]))
    out = jnp.einsum('btd,de->bte', y.astype(h.dtype), w_out)
    return h + out.astype(h.dtype)


def setup_inputs(seed: int = 0) -> dict:
    key = jax.random.key(seed)
    ks = jax.random.split(key, 16)
    f32 = jnp.float32
    x = jax.random.normal(ks[0], (BATCH, SEQ, D_MODEL), f32)
    meta_tokens = jax.random.normal(ks[1], (N_META, D_MODEL), f32)
    norm_w = 1.0 + 0.02 * jax.random.normal(ks[2], (DEPTH, D_MODEL), f32)
    w_in = jax.random.normal(ks[3], (DEPTH, D_MODEL, N_IN), f32) * D_MODEL ** -0.5
    sb_q_norm = 1.0 + 0.02 * jax.random.normal(ks[4], (DEPTH, HEAD_DIM), f32)
    sb_k_norm = 1.0 + 0.02 * jax.random.normal(ks[5], (DEPTH, HEAD_DIM), f32)
    gdn_conv_w = jax.random.normal(ks[6], (DEPTH, CONV_WIDTH, 3 * BRANCH_WIDTH), f32) * CONV_WIDTH ** -0.5
    gdn_a_log = jnp.log(jax.random.uniform(ks[7], (DEPTH, N_HEADS), f32, 1.0, 16.0))
    dt = jnp.exp(jax.random.uniform(ks[8], (DEPTH, N_HEADS), f32) * (math.log(0.1) - math.log(0.001)) + math.log(0.001))
    gdn_dt_bias = dt + jnp.log(-jnp.expm1(-dt))
    gdn_out_norm = 1.0 + 0.02 * jax.random.normal(ks[9], (DEPTH, HEAD_DIM), f32)
    hgrn_lb_logits = jax.random.normal(ks[10], (DEPTH, BRANCH_WIDTH), f32)
    hgrn_out_norm = 1.0 + 0.02 * jax.random.normal(ks[11], (DEPTH, HEAD_DIM), f32)
    w_branch = jax.random.normal(ks[12], (DEPTH, N_BRANCHES, BRANCH_WIDTH, D_MODEL), f32) * BRANCH_WIDTH ** -0.5
    w_out = jax.random.normal(ks[13], (DEPTH, D_MODEL, D_MODEL), f32) * D_MODEL ** -0.5
    return {'x': x, 'meta_tokens': meta_tokens, 'norm_w': norm_w, 'w_in': w_in,
            'sb_q_norm': sb_q_norm, 'sb_k_norm': sb_k_norm, 'gdn_conv_w': gdn_conv_w,
            'gdn_a_log': gdn_a_log, 'gdn_dt_bias': gdn_dt_bias, 'gdn_out_norm': gdn_out_norm,
            'hgrn_lb_logits': hgrn_lb_logits, 'hgrn_out_norm': hgrn_out_norm,
            'w_branch': w_branch, 'w_out': w_out}


def reference(x, meta_tokens, norm_w, w_in, sb_q_norm, sb_k_norm, gdn_conv_w, gdn_a_log,
              gdn_dt_bias, gdn_out_norm, hgrn_lb_logits, hgrn_out_norm, w_branch, w_out):
    bsz = x.shape[0]
    h = jnp.concatenate([
        jnp.zeros((bsz, PAD_FRONT, D_MODEL), x.dtype),
        jnp.broadcast_to(meta_tokens.astype(x.dtype)[None], (bsz, N_META, D_MODEL)),
        x], axis=1)
    t_len = h.shape[1]
    valid = jnp.arange(t_len) >= PAD_FRONT
    p = jax.nn.softmax(hgrn_lb_logits.astype(jnp.float32), axis=0)
    lower_bounds = jnp.cumsum(p, axis=0) - p[0:1]
    for layer in range(DEPTH):
        h = hybrid_layer(h, valid, norm_w[layer], w_in[layer], sb_q_norm[layer], sb_k_norm[layer],
                         gdn_conv_w[layer], gdn_a_log[layer], gdn_dt_bias[layer], gdn_out_norm[layer],
                         lower_bounds[layer], hgrn_out_norm[layer], w_branch[layer], w_out[layer])
    return h[:, FRONT:]
```

```python
import functools

import jax
import jax.numpy as jnp
from jax import lax
from jax.experimental import pallas as pl
from jax.experimental.pallas import tpu as pltpu

F32 = jnp.float32
BF16 = jnp.bfloat16

D_MODEL = 1024
BRANCH_WIDTH = D_MODEL // 2
HEAD_DIM = 128
N_HEADS = BRANCH_WIDTH // HEAD_DIM
N_BRANCHES = 3
CONV_WIDTH = 4
CHUNK = 64
N_META = 16
FRONT = 128
PAD_FRONT = FRONT - N_META
NORM_EPS = 1e-6

LANES = 128
SUBLANES = 8
VMEM_LIMIT_BYTES = 56 * 1024 * 1024

PROJ_ROWS = 640
MERGE_ROWS = 512
SB_TILE = 256
HG_SUB = 8

HIGHEST = lax.Precision.HIGHEST


def _dot(a, b, precision=None):
    return jnp.dot(a, b, preferred_element_type=F32, precision=precision)


def _dot_nt(a, b):
    return lax.dot_general(a, b, (((1,), (1,)), ((), ())), preferred_element_type=F32)


def _dot_tn(a, b):
    return lax.dot_general(a, b, (((0,), (0,)), ((), ())), preferred_element_type=F32)


def _sigmoid(x):
    return 1.0 / (1.0 + jnp.exp(-x))


def _silu(x):
    return x * _sigmoid(x)


def _rms_rows(x, w):
    return x * lax.rsqrt(jnp.mean(x * x, axis=-1, keepdims=True) + NORM_EPS) * w


def _compiler_params(semantics):
    return pltpu.CompilerParams(dimension_semantics=semantics, vmem_limit_bytes=VMEM_LIMIT_BYTES)


def _proj_kernel(h_ref, nw_ref, wqv_ref, wk_ref, wg_ref, wba_ref, whg_ref, qn_ref, kn_ref, conv_ref, gco_ref,
                 qt_ref, k_ref, vt_ref, gqkv_ref, gb_ref, hq_ref, hf_ref, hi_ref, conv_buf):
    t = pl.program_id(1)
    tm = h_ref.shape[1]
    x = h_ref[0]
    xb = _rms_rows(x, nw_ref[...]).astype(BF16)
    pos = t * tm + lax.broadcasted_iota(jnp.int32, (tm, 1), 0)
    vmask = (pos >= PAD_FRONT).astype(F32)

    qv = _dot_nt(wqv_ref[...], xb)
    for hh in range(N_HEADS):
        qh = qv[hh * HEAD_DIM:(hh + 1) * HEAD_DIM]
        ms = jnp.mean(qh * qh, axis=0, keepdims=True)
        qt_ref[0, hh * HEAD_DIM:(hh + 1) * HEAD_DIM, :] = (qh * lax.rsqrt(ms + NORM_EPS) * qn_ref[...]).astype(BF16)
    vt_ref[0] = qv[BRANCH_WIDTH:].astype(BF16)

    kk = _dot(xb, wk_ref[...])
    for hh in range(N_HEADS):
        sl = slice(hh * HEAD_DIM, (hh + 1) * HEAD_DIM)
        k_ref[0, :, sl] = _rms_rows(kk[:, sl], kn_ref[...]).astype(BF16)

    @pl.when(t == 0)
    def _():
        conv_buf[0:SUBLANES, :] = jnp.zeros((SUBLANES, 3 * BRANCH_WIDTH), F32)

    conv_buf[SUBLANES:SUBLANES + tm, :] = _dot(xb, wg_ref[...])
    first = SUBLANES - (CONV_WIDTH - 1)
    y = conv_buf[first:first + tm, :] * conv_ref[0:1, :]
    for i in range(1, CONV_WIDTH):
        y = y + conv_buf[first + i:first + i + tm, :] * conv_ref[i:i + 1, :]
    conv_buf[0:SUBLANES, :] = conv_buf[tm:tm + SUBLANES, :]
    y = _silu(y)
    for j in range(2 * N_HEADS):
        sl = slice(j * HEAD_DIM, (j + 1) * HEAD_DIM)
        yh = y[:, sl]
        yh = yh * lax.rsqrt(jnp.sum(yh * yh, axis=-1, keepdims=True) + NORM_EPS)
        if j < N_HEADS:
            yh = yh * (HEAD_DIM ** -0.5)
        gqkv_ref[0, :, sl] = yh.astype(BF16)
    gqkv_ref[0, :, 2 * BRANCH_WIDTH:] = y[:, 2 * BRANCH_WIDTH:].astype(BF16)

    ba = _dot(xb, wba_ref[...])
    sp_in = ba + gco_ref[1:2, :]
    softplus = jnp.maximum(sp_in, 0.0) + jnp.log(1.0 + jnp.exp(-jnp.abs(sp_in)))
    lane = lax.broadcasted_iota(jnp.int32, ba.shape, 1)
    gb_ref[0] = jnp.where(lane < N_HEADS, _sigmoid(ba) * vmask, gco_ref[0:1, :] * softplus)

    hg = _dot(xb, whg_ref[...])
    hq_ref[0] = _silu(hg[:, :BRANCH_WIDTH]).astype(BF16)
    hf_ref[0] = hg[:, BRANCH_WIDTH:2 * BRANCH_WIDTH]
    hi_ref[0] = (hg[:, 2 * BRANCH_WIDTH:] * vmask).astype(BF16)


def _proj_call(h, lw):
    bsz, t_len, d = h.shape
    tm = PROJ_ROWS
    assert t_len % tm == 0
    full = lambda a: pl.BlockSpec(a.shape, lambda b, t: (0,) * a.ndim)
    row = lambda w: pl.BlockSpec((1, tm, w), lambda b, t: (b, t, 0))
    col = lambda w: pl.BlockSpec((1, w, tm), lambda b, t: (b, 0, t))
    weights = (lw["norm_w"], lw["w_qv_t"], lw["w_k"], lw["w_gqkv"], lw["w_ba"], lw["w_hg"], lw["sb_qn"], lw["sb_kn"],
               lw["conv_w"], lw["g_coef"])
    bw = BRANCH_WIDTH
    out_shape = (
        jax.ShapeDtypeStruct((bsz, bw, t_len), BF16),
        jax.ShapeDtypeStruct((bsz, t_len, bw), BF16),
        jax.ShapeDtypeStruct((bsz, bw, t_len), BF16),
        jax.ShapeDtypeStruct((bsz, t_len, 3 * bw), BF16),
        jax.ShapeDtypeStruct((bsz, t_len, LANES), F32),
        jax.ShapeDtypeStruct((bsz, t_len, bw), BF16),
        jax.ShapeDtypeStruct((bsz, t_len, bw), F32),
        jax.ShapeDtypeStruct((bsz, t_len, bw), BF16),
    )
    out_specs = (col(bw), row(bw), col(bw), row(3 * bw), row(LANES), row(bw), row(bw), row(bw))
    return pl.pallas_call(
        _proj_kernel,
        grid=(bsz, t_len // tm),
        in_specs=[row(d)] + [full(w) for w in weights],
        out_specs=out_specs,
        out_shape=out_shape,
        scratch_shapes=[pltpu.VMEM((tm + SUBLANES, 3 * bw), F32)],
        compiler_params=_compiler_params(("arbitrary", "arbitrary")),
        name="proj",
    )(h, *weights)


def _sb_kernel(qt_ref, k_ref, vt_ref, o_ref, acc_ref, carry_ref):
    t_len = k_ref.shape[1]
    n_tiles = (t_len - FRONT) // SB_TILE
    ii = lax.broadcasted_iota(jnp.int32, (SB_TILE, SB_TILE), 0)
    jj = lax.broadcasted_iota(jnp.int32, (SB_TILE, SB_TILE), 1)
    later = (jj > ii).astype(BF16)

    def tile(q0, nq, k0, nk, causal, front):
        z = _dot(k_ref[0, pl.ds(k0, nk), :], qt_ref[0, :, pl.ds(q0, nq)])
        nz = -z
        log_keep = jnp.minimum(nz, 0.0) - jnp.log(1.0 + jnp.exp(jnp.minimum(z, nz)))
        log_beta = z + log_keep
        mask = None
        if causal or front:
            s_idx = k0 + lax.broadcasted_iota(jnp.int32, (nk, nq), 0)
            if causal:
                mask = s_idx < q0 + lax.broadcasted_iota(jnp.int32, (nk, nq), 1)
            if front:
                valid = s_idx >= PAD_FRONT
                mask = valid if mask is None else mask & valid
            log_keep = jnp.where(mask, log_keep, 0.0)
        hi = log_keep.astype(BF16)
        lo = (log_keep - hi.astype(F32)).astype(BF16)
        lt = later[:nk, :nk]
        passed = _dot(lt, hi) + _dot(lt, lo)
        attn = jnp.exp(log_beta + passed + carry_ref[:, :nq])
        if mask is not None:
            attn = jnp.where(mask, attn, 0.0)
        acc_ref[:, :nq] += _dot(vt_ref[0, :, pl.ds(k0, nk)], attn.astype(BF16))
        carry_ref[:, :nq] += passed[0:1, :] + log_keep[0:1, :]

    def finish(q0, nq):
        o_ref[0, pl.ds(q0, nq), :] = acc_ref[:, :nq].T.astype(o_ref.dtype)

    acc_ref[...] = jnp.zeros_like(acc_ref)
    carry_ref[...] = jnp.zeros_like(carry_ref)
    tile(0, FRONT, 0, FRONT, True, True)
    finish(0, FRONT)

    def q_body(m, _):
        q0 = pl.multiple_of(FRONT + m * SB_TILE, LANES)
        acc_ref[...] = jnp.zeros_like(acc_ref)
        carry_ref[...] = jnp.zeros_like(carry_ref)
        tile(q0, SB_TILE, q0, SB_TILE, True, False)

        def k_body(i, _):
            k0 = pl.multiple_of(q0 - (i + 1) * SB_TILE, LANES)
            tile(q0, SB_TILE, k0, SB_TILE, False, False)
            return 0

        lax.fori_loop(0, m, k_body, 0)
        tile(q0, SB_TILE, 0, FRONT, False, True)
        finish(q0, SB_TILE)
        return 0

    lax.fori_loop(0, n_tiles, q_body, 0)


def _sb_call(qt, k, vt):
    bsz, t_len, bw = k.shape
    assert (t_len - FRONT) % SB_TILE == 0
    col = pl.BlockSpec((1, HEAD_DIM, t_len), lambda b, h: (b, h, 0))
    row = pl.BlockSpec((1, t_len, HEAD_DIM), lambda b, h: (b, 0, h))
    return pl.pallas_call(
        _sb_kernel,
        grid=(bsz, N_HEADS),
        in_specs=[col, row, col],
        out_specs=row,
        out_shape=jax.ShapeDtypeStruct((bsz, t_len, bw), BF16),
        scratch_shapes=[pltpu.VMEM((HEAD_DIM, SB_TILE), F32), pltpu.VMEM((1, SB_TILE), F32)],
        compiler_params=_compiler_params(("arbitrary", "arbitrary")),
        name="stick_breaking",
    )(qt, k, vt)


def _chunk_consts():
    ii = lax.broadcasted_iota(jnp.int32, (CHUNK, CHUNK), 0)
    jj = lax.broadcasted_iota(jnp.int32, (CHUNK, CHUNK), 1)
    return ii, jj


def _gdn_kernel(gqkv_ref, gb_ref, on_ref, o_ref, s_ref):
    t = pl.program_id(1)
    tm = gqkv_ref.shape[1]
    bw = BRANCH_WIDTH

    @pl.when(t == 0)
    def _():
        s_ref[...] = jnp.zeros_like(s_ref)

    ii, jj = _chunk_consts()
    incl = (ii >= jj).astype(F32)
    eye = (ii == jj).astype(F32)
    ones = jnp.ones((CHUNK, CHUNK), F32)
    causal = ii >= jj
    strict = ii > jj

    def chunk_body(c, _):
        r0 = pl.multiple_of(c * CHUNK, CHUNK)
        gbc = gb_ref[0, pl.ds(r0, CHUNK), :]
        g_cum = _dot(incl, gbc, HIGHEST)
        for hh in range(N_HEADS):
            sl = slice(hh * HEAD_DIM, (hh + 1) * HEAD_DIM)
            q = gqkv_ref[0, pl.ds(r0, CHUNK), hh * HEAD_DIM:(hh + 1) * HEAD_DIM].astype(F32)
            k = gqkv_ref[0, pl.ds(r0, CHUNK), bw + hh * HEAD_DIM:bw + (hh + 1) * HEAD_DIM].astype(F32)
            v = gqkv_ref[0, pl.ds(r0, CHUNK), 2 * bw + hh * HEAD_DIM:2 * bw + (hh + 1) * HEAD_DIM].astype(F32)
            beta = gbc[:, hh:hh + 1]
            gc = g_cum[:, N_HEADS + hh:N_HEADS + hh + 1]
            g_cols = _dot(ones, eye * gc, HIGHEST)
            decay = jnp.exp(jnp.minimum(gc - g_cols, 0.0))
            kb = k * beta
            kbf = k.astype(BF16)
            m = jnp.where(strict, _dot_nt(kb.astype(BF16), kbf) * decay, 0.0)
            t_inv = eye - m
            p = m
            for _ in range(5):
                p = _dot(p, p, HIGHEST)
                t_inv = t_inv + _dot(t_inv, p, HIGHEST)
            eg = jnp.exp(gc)
            g_last = gc[CHUNK - 1:CHUNK, :]
            u = _dot(t_inv, v * beta, HIGHEST)
            w = _dot(t_inv, kb * eg, HIGHEST)
            a_qk = jnp.where(causal, _dot_nt(q.astype(BF16), kbf) * decay, 0.0)
            q_dec = q * eg
            k_dec = k * jnp.exp(g_last - gc)
            state = s_ref[hh]
            sb = state.astype(BF16)
            v_new = u - _dot(w.astype(BF16), sb)
            vb = v_new.astype(BF16)
            o = _dot(q_dec.astype(BF16), sb) + _dot(a_qk.astype(BF16), vb)
            s_ref[hh] = state * jnp.exp(g_last) + _dot_tn(k_dec.astype(BF16), vb)
            o_ref[0, pl.ds(r0, CHUNK), sl] = _rms_rows(o, on_ref[...]).astype(o_ref.dtype)
        return 0

    lax.fori_loop(0, tm // CHUNK, chunk_body, 0)


def _gdn_call(gqkv, gb, out_norm):
    bsz, t_len, _ = gqkv.shape
    tm = PROJ_ROWS
    row = lambda w: pl.BlockSpec((1, tm, w), lambda b, t: (b, t, 0))
    return pl.pallas_call(
        _gdn_kernel,
        grid=(bsz, t_len // tm),
        in_specs=[row(3 * BRANCH_WIDTH), row(LANES), pl.BlockSpec(out_norm.shape, lambda b, t: (0, 0))],
        out_specs=row(BRANCH_WIDTH),
        out_shape=jax.ShapeDtypeStruct((bsz, t_len, BRANCH_WIDTH), BF16),
        scratch_shapes=[pltpu.VMEM((N_HEADS, HEAD_DIM, HEAD_DIM), F32)],
        compiler_params=_compiler_params(("arbitrary", "arbitrary")),
        name="gated_deltanet",
    )(gqkv, gb, out_norm)


def _hgrn_kernel(hq_ref, hf_ref, hi_ref, lb_ref, on_ref, o_ref, s_ref, diag_ref):
    t = pl.program_id(1)
    tm = hq_ref.shape[1]

    @pl.when(t == 0)
    def _():
        s_ref[...] = jnp.zeros_like(s_ref)

    ii, jj = _chunk_consts()
    incl = (ii >= jj).astype(F32)
    levels = []
    half = CHUNK // 2
    while half >= HG_SUB:
        row_upper = (ii % (2 * half)) >= half
        same_group = (ii // (2 * half)) == (jj // (2 * half))
        col_lower = (jj % (2 * half)) < half
        levels.append((half, row_upper & same_group & col_lower))
        half //= 2
    sub_j = lax.broadcasted_iota(jnp.int32, (HG_SUB, 1), 0)

    def chunk_body(c, _):
        r0 = pl.multiple_of(c * CHUNK, CHUNK)
        for hh in range(N_HEADS):
            sl = slice(hh * HEAD_DIM, (hh + 1) * HEAD_DIM)
            f_pre = hf_ref[0, pl.ds(r0, CHUNK), hh * HEAD_DIM:(hh + 1) * HEAD_DIM]
            lbh = lb_ref[:, sl]
            forget = lbh + (1.0 - lbh) * _sigmoid(f_pre)
            k = (1.0 - lbh) * _sigmoid(-f_pre)
            g_cum = _dot(incl, jnp.log(forget), HIGHEST)
            q = hq_ref[0, pl.ds(r0, CHUNK), hh * HEAD_DIM:(hh + 1) * HEAD_DIM].astype(F32)
            vb = hi_ref[0, pl.ds(r0, CHUNK), hh * HEAD_DIM:(hh + 1) * HEAD_DIM]
            v = vb.astype(F32)
            g_end = g_cum[CHUNK - 1:CHUNK, :]
            state_t = s_ref[hh]
            o = _dot_nt((q * jnp.exp(g_cum)).astype(BF16), state_t.astype(BF16))

            a_off = jnp.zeros((CHUNK, CHUNK), F32)
            for half, lvl_mask in levels:
                ref_rows = jnp.concatenate(
                    [jnp.broadcast_to(g_cum[g0 + half:g0 + half + 1, :], (2 * half, HEAD_DIM))
                     for g0 in range(0, CHUNK, 2 * half)], axis=0)
                qs = q * jnp.exp(jnp.minimum(g_cum - ref_rows, 0.0))
                ks = k * jnp.exp(jnp.minimum(ref_rows - g_cum, 0.0))
                a_off = a_off + jnp.where(lvl_mask, _dot_nt(qs.astype(BF16), ks.astype(BF16)), 0.0)
            o = o + _dot(a_off.astype(BF16), vb)

            for i in range(CHUNK):
                b0 = (i // HG_SUB) * HG_SUB
                dec = jnp.exp(jnp.minimum(g_cum[i:i + 1, :] - g_cum[b0:b0 + HG_SUB, :], 0.0))
                a_col = jnp.sum(dec * k[b0:b0 + HG_SUB, :] * q[i:i + 1, :], axis=-1, keepdims=True)
                a_col = jnp.where(sub_j <= (i - b0), a_col, 0.0)
                diag_ref[i:i + 1, :] = jnp.sum(a_col * v[b0:b0 + HG_SUB, :], axis=0, keepdims=True)
            o = o + diag_ref[...]

            k_dec = k * jnp.exp(g_end - g_cum)
            s_ref[hh] = state_t * jnp.exp(g_end) + _dot_tn(vb, k_dec.astype(BF16))
            o_ref[0, pl.ds(r0, CHUNK), sl] = _rms_rows(o, on_ref[...]).astype(o_ref.dtype)
        return 0

    lax.fori_loop(0, tm // CHUNK, chunk_body, 0)


def _hgrn_call(hq, hf, hi, lower_bound, out_norm):
    bsz, t_len, bw = hq.shape
    tm = PROJ_ROWS
    row = pl.BlockSpec((1, tm, bw), lambda b, t: (b, t, 0))
    full = lambda a: pl.BlockSpec(a.shape, lambda b, t: (0, 0))
    return pl.pallas_call(
        _hgrn_kernel,
        grid=(bsz, t_len // tm),
        in_specs=[row, row, row, full(lower_bound), full(out_norm)],
        out_specs=row,
        out_shape=jax.ShapeDtypeStruct((bsz, t_len, bw), BF16),
        scratch_shapes=[pltpu.VMEM((N_HEADS, HEAD_DIM, HEAD_DIM), F32), pltpu.VMEM((CHUNK, HEAD_DIM), F32)],
        compiler_params=_compiler_params(("arbitrary", "arbitrary")),
        name="hgrn2",
    )(hq, hf, hi, lower_bound, out_norm)


def _merge_kernel(h_ref, osb_ref, ogd_ref, ohg_ref, nw_ref, wz_ref, wmix_ref, wb_ref, wo_ref, out_ref):
    x = h_ref[...]
    xb = _rms_rows(x, nw_ref[...]).astype(BF16)
    y = None
    for b, o_ref in enumerate((osb_ref, ogd_ref, ohg_ref)):
        zg = _dot(xb, wz_ref[:, b * BRANCH_WIDTH:(b + 1) * BRANCH_WIDTH])
        gated = (o_ref[...].astype(F32) * _silu(zg)).astype(BF16)
        mix = _sigmoid(_dot(xb, wmix_ref[:, b * D_MODEL:(b + 1) * D_MODEL]))
        term = mix * _dot(gated, wb_ref[b])
        y = term if y is None else y + term
    out_ref[...] = x + _dot(y.astype(BF16), wo_ref[...])


def _merge_call(h2, osb, ogd, ohg, lw):
    rows, d = h2.shape
    tm = MERGE_ROWS
    assert rows % tm == 0
    row = lambda w: pl.BlockSpec((tm, w), lambda i: (i, 0))
    full = lambda a: pl.BlockSpec(a.shape, lambda i: (0,) * a.ndim)
    weights = (lw["norm_w"], lw["w_z"], lw["w_mix"], lw["w_branch"], lw["w_out"])
    return pl.pallas_call(
        _merge_kernel,
        grid=(rows // tm,),
        in_specs=[row(d), row(BRANCH_WIDTH), row(BRANCH_WIDTH), row(BRANCH_WIDTH)] + [full(w) for w in weights],
        out_specs=row(d),
        out_shape=jax.ShapeDtypeStruct((rows, d), F32),
        compiler_params=_compiler_params(("arbitrary",)),
        name="merge",
    )(h2, osb, ogd, ohg, *weights)


def _layer_weights(layer, norm_w, w_in, sb_q_norm, sb_k_norm, gdn_conv_w, gdn_a_log, gdn_dt_bias, w_branch, w_out):
    bw = BRANCH_WIDTH
    w = w_in[layer]
    o_gqkv = 4 * bw
    o_gz = o_gqkv + 3 * bw
    o_gb = o_gz + bw
    o_hq = o_gb + 2 * N_HEADS
    o_hz = o_hq + 3 * bw
    o_mix = o_hz + bw
    w_ba = jnp.zeros((D_MODEL, LANES), F32).at[:, :2 * N_HEADS].set(w[:, o_gb:o_gb + 2 * N_HEADS])
    pad = LANES - 2 * N_HEADS
    neg_a = jnp.concatenate([jnp.zeros((N_HEADS,), F32), -jnp.exp(gdn_a_log[layer].astype(F32)), jnp.zeros((pad,), F32)])
    dt_b = jnp.concatenate([jnp.zeros((N_HEADS,), F32), gdn_dt_bias[layer].astype(F32), jnp.zeros((pad,), F32)])
    return {
        "norm_w": norm_w[layer].astype(F32)[None, :],
        "w_qv_t": jnp.concatenate([w[:, 0:bw], w[:, 2 * bw:3 * bw]], axis=1).T.astype(BF16),
        "w_k": w[:, bw:2 * bw].astype(BF16),
        "w_gqkv": w[:, o_gqkv:o_gqkv + 3 * bw].astype(BF16),
        "w_ba": w_ba.astype(BF16),
        "w_hg": w[:, o_hq:o_hq + 3 * bw].astype(BF16),
        "sb_qn": (sb_q_norm[layer].astype(F32) * (HEAD_DIM ** -0.5))[:, None],
        "sb_kn": sb_k_norm[layer].astype(F32)[None, :],
        "conv_w": gdn_conv_w[layer].astype(F32),
        "g_coef": jnp.stack([neg_a, dt_b]),
        "w_z": jnp.concatenate([w[:, 3 * bw:4 * bw], w[:, o_gz:o_gz + bw], w[:, o_hz:o_hz + bw]], axis=1).astype(BF16),
        "w_mix": w[:, o_mix:].astype(BF16),
        "w_branch": w_branch[layer].astype(BF16),
        "w_out": w_out[layer].astype(BF16),
    }


def kernel(x, meta_tokens, norm_w, w_in, sb_q_norm, sb_k_norm, gdn_conv_w, gdn_a_log, gdn_dt_bias, gdn_out_norm,
           hgrn_lb_logits, hgrn_out_norm, w_branch, w_out):
    bsz, _, d = x.shape
    depth = w_in.shape[0]
    h = jnp.concatenate([
        jnp.zeros((bsz, PAD_FRONT, d), x.dtype),
        jnp.broadcast_to(meta_tokens.astype(x.dtype)[None], (bsz, N_META, d)),
        x], axis=1)
    t_len = h.shape[1]
    p = jax.nn.softmax(hgrn_lb_logits.astype(F32), axis=0)
    lower_bounds = jnp.cumsum(p, axis=0) - p[0:1]
    for layer in range(depth):
        lw = _layer_weights(layer, norm_w, w_in, sb_q_norm, sb_k_norm, gdn_conv_w, gdn_a_log, gdn_dt_bias,
                            w_branch, w_out)
        qt, k, vt, gqkv, gb, hq, hf, hi = _proj_call(h, lw)
        o_sb = _sb_call(qt, k, vt)
        o_gd = _gdn_call(gqkv, gb, gdn_out_norm[layer].astype(F32)[None, :])
        o_hg = _hgrn_call(hq, hf, hi, lower_bounds[layer][None, :], hgrn_out_norm[layer].astype(F32)[None, :])
        flat = lambda a: a.reshape(bsz * t_len, a.shape[-1])
        h = _merge_call(flat(h), flat(o_sb), flat(o_gd), flat(o_hg), lw).reshape(bsz, t_len, d)
    return h[:, FRONT:]
```

```python
import functools

import jax
import jax.numpy as jnp
from jax import lax
from jax.experimental import pallas as pl
from jax.experimental.pallas import tpu as pltpu

F32 = jnp.float32
BF16 = jnp.bfloat16

D_MODEL = 1024
BRANCH_WIDTH = D_MODEL // 2
HEAD_DIM = 128
N_HEADS = BRANCH_WIDTH // HEAD_DIM
N_BRANCHES = 3
CONV_WIDTH = 4
CHUNK = 64
N_META = 16
FRONT = 128
PAD_FRONT = FRONT - N_META
NORM_EPS = 1e-6

LANES = 128
SUBLANES = 8
VMEM_LIMIT_BYTES = 56 * 1024 * 1024

PROJ_ROWS = 640
MERGE_ROWS = 512
SB_TILE = 256
SB_SUB = 128
SB_HEADS = 2
PAIR = 2 * CHUNK
HG_SUB = 8
LOG2E = 1.4426950408889634


def _dot(a, b):
    return jnp.dot(a, b, preferred_element_type=F32)


def _dot_nt(a, b):
    return lax.dot_general(a, b, (((1,), (1,)), ((), ())), preferred_element_type=F32)


def _dot_tn(a, b):
    return lax.dot_general(a, b, (((0,), (0,)), ((), ())), preferred_element_type=F32)


def _sigmoid(x):
    return 1.0 / (1.0 + jnp.exp(-x))


def _silu(x):
    return x * _sigmoid(x)


def _rms_rows(x, w):
    return x * lax.rsqrt(jnp.mean(x * x, axis=-1, keepdims=True) + NORM_EPS) * w


def _compiler_params(semantics):
    return pltpu.CompilerParams(dimension_semantics=semantics, vmem_limit_bytes=VMEM_LIMIT_BYTES)


def _proj_kernel(h_ref, nw_ref, wqv_ref, wk_ref, wg_ref, wba_ref, whg_ref, qn_ref, kn_ref, conv_ref, gco_ref,
                 qt_ref, k_ref, vt_ref, gqkv_ref, gb_ref, hq_ref, hf_ref, hi_ref, conv_buf):
    t = pl.program_id(1)
    tm = h_ref.shape[1]
    x = h_ref[0]
    xb = _rms_rows(x, nw_ref[...]).astype(BF16)
    pos = t * tm + lax.broadcasted_iota(jnp.int32, (tm, 1), 0)
    vmask = (pos >= PAD_FRONT).astype(F32)

    qv = _dot_nt(wqv_ref[...], xb)
    for hh in range(N_HEADS):
        qh = qv[hh * HEAD_DIM:(hh + 1) * HEAD_DIM]
        ms = jnp.mean(qh * qh, axis=0, keepdims=True)
        qt_ref[0, hh * HEAD_DIM:(hh + 1) * HEAD_DIM, :] = (qh * lax.rsqrt(ms + NORM_EPS) * qn_ref[...]).astype(BF16)
    vt_ref[0] = qv[BRANCH_WIDTH:].astype(BF16)

    kk = _dot(xb, wk_ref[...])
    for hh in range(N_HEADS):
        sl = slice(hh * HEAD_DIM, (hh + 1) * HEAD_DIM)
        k_ref[0, :, sl] = _rms_rows(kk[:, sl], kn_ref[...]).astype(BF16)

    @pl.when(t == 0)
    def _():
        conv_buf[0:SUBLANES, :] = jnp.zeros((SUBLANES, 3 * BRANCH_WIDTH), F32)

    conv_buf[SUBLANES:SUBLANES + tm, :] = _dot(xb, wg_ref[...])
    first = SUBLANES - (CONV_WIDTH - 1)
    y = conv_buf[first:first + tm, :] * conv_ref[0:1, :]
    for i in range(1, CONV_WIDTH):
        y = y + conv_buf[first + i:first + i + tm, :] * conv_ref[i:i + 1, :]
    conv_buf[0:SUBLANES, :] = conv_buf[tm:tm + SUBLANES, :]
    y = _silu(y)
    for j in range(2 * N_HEADS):
        sl = slice(j * HEAD_DIM, (j + 1) * HEAD_DIM)
        yh = y[:, sl]
        yh = yh * lax.rsqrt(jnp.sum(yh * yh, axis=-1, keepdims=True) + NORM_EPS)
        if j < N_HEADS:
            yh = yh * (HEAD_DIM ** -0.5)
        gqkv_ref[0, :, sl] = yh.astype(BF16)
    gqkv_ref[0, :, 2 * BRANCH_WIDTH:] = y[:, 2 * BRANCH_WIDTH:].astype(BF16)

    ba = _dot(xb, wba_ref[...])
    sp_in = ba + gco_ref[1:2, :]
    softplus = jnp.maximum(sp_in, 0.0) + jnp.log(1.0 + jnp.exp(-jnp.abs(sp_in)))
    lane = lax.broadcasted_iota(jnp.int32, ba.shape, 1)
    gb_ref[0] = jnp.where(lane < N_HEADS, _sigmoid(ba) * vmask, gco_ref[0:1, :] * softplus)

    hg = _dot(xb, whg_ref[...])
    hq_ref[0] = _silu(hg[:, :BRANCH_WIDTH]).astype(BF16)
    hf_ref[0] = hg[:, BRANCH_WIDTH:2 * BRANCH_WIDTH]
    hi_ref[0] = (hg[:, 2 * BRANCH_WIDTH:] * vmask).astype(BF16)


def _proj_call(h, lw):
    bsz, t_len, d = h.shape
    tm = PROJ_ROWS
    assert t_len % tm == 0
    full = lambda a: pl.BlockSpec(a.shape, lambda b, t: (0,) * a.ndim)
    row = lambda w: pl.BlockSpec((1, tm, w), lambda b, t: (b, t, 0))
    col = lambda w: pl.BlockSpec((1, w, tm), lambda b, t: (b, 0, t))
    weights = (lw["norm_w"], lw["w_qv_t"], lw["w_k"], lw["w_gqkv"], lw["w_ba"], lw["w_hg"], lw["sb_qn"], lw["sb_kn"],
               lw["conv_w"], lw["g_coef"])
    bw = BRANCH_WIDTH
    out_shape = (
        jax.ShapeDtypeStruct((bsz, bw, t_len), BF16),
        jax.ShapeDtypeStruct((bsz, t_len, bw), BF16),
        jax.ShapeDtypeStruct((bsz, bw, t_len), BF16),
        jax.ShapeDtypeStruct((bsz, t_len, 3 * bw), BF16),
        jax.ShapeDtypeStruct((bsz, t_len, LANES), F32),
        jax.ShapeDtypeStruct((bsz, t_len, bw), BF16),
        jax.ShapeDtypeStruct((bsz, t_len, bw), F32),
        jax.ShapeDtypeStruct((bsz, t_len, bw), BF16),
    )
    out_specs = (col(bw), row(bw), col(bw), row(3 * bw), row(LANES), row(bw), row(bw), row(bw))
    return pl.pallas_call(
        _proj_kernel,
        grid=(bsz, t_len // tm),
        in_specs=[row(d)] + [full(w) for w in weights],
        out_specs=out_specs,
        out_shape=out_shape,
        scratch_shapes=[pltpu.VMEM((tm + SUBLANES, 3 * bw), F32)],
        compiler_params=_compiler_params(("arbitrary", "arbitrary")),
        name="proj",
    )(h, *weights)


def _sb_kernel(qt_ref, k_ref, vt_ref, o_ref, acc_ref, carry_ref):
    t_len = k_ref.shape[1]
    n_tiles = (t_len - FRONT) // SB_TILE
    ii = lax.broadcasted_iota(jnp.int32, (SB_SUB, SB_SUB), 0)
    jj = lax.broadcasted_iota(jnp.int32, (SB_SUB, SB_SUB), 1)
    later = (jj > ii).astype(BF16)

    def tiles(q0, nq, k_lo, widths, causal, front):
        heads = range(SB_HEADS)
        hs = lambda hd: slice(hd * HEAD_DIM, (hd + 1) * HEAD_DIM)
        starts = [sum(widths[:i]) for i in range(len(widths))]
        total = sum(widths)
        order = [(bi, hd) for bi in reversed(range(len(widths))) for hd in heads]
        z = {}
        for bi, hd in order:
            rows = pl.ds(k_lo + starts[bi], widths[bi])
            z[bi, hd] = _dot(k_ref[0, rows, hs(hd)], qt_ref[0, hs(hd), pl.ds(q0, nq)])
        mask = {}
        for bi in range(len(widths)):
            mask[bi] = None
            if causal or front:
                s_idx = k_lo + starts[bi] + lax.broadcasted_iota(jnp.int32, (widths[bi], nq), 0)
                if causal:
                    mask[bi] = s_idx < q0 + lax.broadcasted_iota(jnp.int32, (widths[bi], nq), 1)
                if front:
                    valid = s_idx >= PAD_FRONT
                    mask[bi] = valid if mask[bi] is None else mask[bi] & valid
        log_beta, keep = {}, {}
        for key in order:
            zz = z[key]
            nz = -zz
            log_keep = jnp.minimum(nz, 0.0) - jnp.log(1.0 + jnp.exp2(jnp.minimum(zz, nz))) * LOG2E
            log_beta[key] = zz + log_keep
            if mask[key[0]] is not None:
                log_keep = jnp.where(mask[key[0]], log_keep, 0.0)
            keep[key] = log_keep.astype(BF16)
        passed = {}
        for bi, hd in order:
            for s in reversed(range(0, widths[bi], SB_SUB)):
                passed[bi, hd, s] = _dot(later, keep[bi, hd][s:s + SB_SUB])
        for hd in heads:
            carry = carry_ref[hd, :, :nq]
            parts = []
            for bi in reversed(range(len(widths))):
                for s in reversed(range(0, widths[bi], SB_SUB)):
                    p = passed[bi, hd, s]
                    attn = jnp.exp2(log_beta[bi, hd][s:s + SB_SUB] + p + carry)
                    if mask[bi] is not None:
                        attn = jnp.where(mask[bi][s:s + SB_SUB], attn, 0.0)
                    parts.insert(0, attn.astype(BF16))
                    carry = carry + p[0:1, :] + keep[bi, hd][s:s + 1, :].astype(F32)
            carry_ref[hd, :, :nq] = carry
            attn_all = parts[0] if len(parts) == 1 else jnp.concatenate(parts, axis=0)
            acc_ref[hd, :, :nq] += _dot(vt_ref[0, hs(hd), pl.ds(k_lo, total)], attn_all)

    def start():
        acc_ref[...] = jnp.zeros_like(acc_ref)
        carry_ref[...] = jnp.zeros_like(carry_ref)

    def finish(q0, nq):
        for hd in range(SB_HEADS):
            o_ref[0, pl.ds(q0, nq), hd * HEAD_DIM:(hd + 1) * HEAD_DIM] = acc_ref[hd, :, :nq].T.astype(o_ref.dtype)

    start()
    tiles(0, FRONT, 0, [FRONT], True, True)
    finish(0, FRONT)

    def q_body(m, _):
        q0 = pl.multiple_of(FRONT + m * SB_TILE, LANES)
        start()
        tiles(q0, SB_TILE, q0, [SB_TILE], True, False)

        def k_pair(i, _):
            k_lo = pl.multiple_of(q0 - (2 * i + 2) * SB_TILE, LANES)
            tiles(q0, SB_TILE, k_lo, [SB_TILE, SB_TILE], False, False)
            return 0

        lax.fori_loop(0, lax.shift_right_logical(m, 1), k_pair, 0)

        @pl.when((m & 1) == 1)
        def _():
            tiles(q0, SB_TILE, FRONT, [SB_TILE], False, False)

        tiles(q0, SB_TILE, 0, [FRONT], False, True)
        finish(q0, SB_TILE)
        return 0

    lax.fori_loop(0, n_tiles, q_body, 0)


def _sb_call(qt, k, vt):
    bsz, t_len, bw = k.shape
    assert (t_len - FRONT) % SB_TILE == 0 and N_HEADS % SB_HEADS == 0
    hw = SB_HEADS * HEAD_DIM
    col = pl.BlockSpec((1, hw, t_len), lambda b, h: (b, h, 0))
    row = pl.BlockSpec((1, t_len, hw), lambda b, h: (b, 0, h))
    return pl.pallas_call(
        _sb_kernel,
        grid=(bsz, N_HEADS // SB_HEADS),
        in_specs=[col, row, col],
        out_specs=row,
        out_shape=jax.ShapeDtypeStruct((bsz, t_len, bw), BF16),
        scratch_shapes=[pltpu.VMEM((SB_HEADS, HEAD_DIM, SB_TILE), F32), pltpu.VMEM((SB_HEADS, 1, SB_TILE), F32)],
        compiler_params=_compiler_params(("arbitrary", "arbitrary")),
        name="stick_breaking",
    )(qt, k, vt)


def _chunk_consts():
    ii = lax.broadcasted_iota(jnp.int32, (CHUNK, CHUNK), 0)
    jj = lax.broadcasted_iota(jnp.int32, (CHUNK, CHUNK), 1)
    return ii, jj


def _split3(x):
    hi = x.astype(BF16)
    r = x - hi.astype(F32)
    mid = r.astype(BF16)
    return hi, mid, (r - mid.astype(F32)).astype(BF16)


def _dot_01(tri, x):
    hi, mid, lo = _split3(x)
    return _dot(tri, hi) + _dot(tri, mid) + _dot(tri, lo)


def _pair_masks():
    ii = lax.broadcasted_iota(jnp.int32, (PAIR, PAIR), 0)
    jj = lax.broadcasted_iota(jnp.int32, (PAIR, PAIR), 1)
    same = (ii // CHUNK) == (jj // CHUNK)
    return same & (ii >= jj), same & (ii > jj)


def _gdn_kernel(gqkv_ref, gb_ref, on_ref, o_ref, s_ref):
    t = pl.program_id(1)
    tm = gqkv_ref.shape[1]
    bw = BRANCH_WIDTH

    @pl.when(t == 0)
    def _():
        s_ref[...] = jnp.zeros_like(s_ref)

    causal, strict = _pair_masks()
    incl = causal.astype(BF16)
    second = lax.broadcasted_iota(jnp.int32, (PAIR, 1), 0) >= CHUNK

    def pair_body(d, _):
        rows = pl.ds(pl.multiple_of(d * PAIR, PAIR), PAIR)
        gbc = gb_ref[0, rows, :]
        g_cum = _dot_01(incl, gbc)
        g_cum_t = g_cum.T
        qs = [gqkv_ref[0, rows, hh * HEAD_DIM:(hh + 1) * HEAD_DIM] for hh in range(N_HEADS)]
        ks = [gqkv_ref[0, rows, bw + hh * HEAD_DIM:bw + (hh + 1) * HEAD_DIM] for hh in range(N_HEADS)]
        vs = [gqkv_ref[0, rows, 2 * bw + hh * HEAD_DIM:2 * bw + (hh + 1) * HEAD_DIM] for hh in range(N_HEADS)]
        heads = range(N_HEADS)
        kf = [k.astype(F32) for k in ks]
        beta = [gbc[:, hh:hh + 1] for hh in heads]
        gc = [g_cum[:, N_HEADS + hh:N_HEADS + hh + 1] for hh in heads]
        decay = [jnp.exp(jnp.minimum(gc[hh] - g_cum_t[N_HEADS + hh:N_HEADS + hh + 1, :], 0.0))
                 for hh in heads]
        kb = [kf[hh] * beta[hh] for hh in heads]
        m = [jnp.where(strict, _dot_nt(kb[hh].astype(BF16), ks[hh]) * decay[hh], 0.0) for hh in heads]
        a_qk = [jnp.where(causal, _dot_nt(qs[hh], ks[hh]) * decay[hh], 0.0).astype(BF16) for hh in heads]
        e = [-mm for mm in m]
        pb = [mm.astype(BF16) for mm in m]
        for _ in range(5):
            p = [_dot(pb[hh], pb[hh]) for hh in heads]
            pb = [pp.astype(BF16) for pp in p]
            e = [e[hh] + p[hh] + _dot(e[hh].astype(BF16), pb[hh]) for hh in heads]
        eg = [jnp.exp(g) for g in gc]
        rhs = [jnp.concatenate([vs[hh].astype(F32) * beta[hh], kb[hh] * eg[hh]], axis=1) for hh in heads]
        uwb = [(rhs[hh] + _dot(e[hh].astype(BF16), rhs[hh].astype(BF16))).astype(BF16) for hh in heads]
        a_uw = [_dot(a_qk[hh], uwb[hh]) for hh in heads]
        q_eff = [(qs[hh].astype(F32) * eg[hh] - a_uw[hh][:, HEAD_DIM:]).astype(BF16) for hh in heads]
        k_dec = []
        for hh in heads:
            g_last = jnp.where(second, gc[hh][PAIR - 1:PAIR, :], gc[hh][CHUNK - 1:CHUNK, :])
            k_dec.append((kf[hh] * jnp.exp(g_last - gc[hh])).astype(BF16))
        state = [s_ref[hh] for hh in heads]
        outs = [[] for _ in heads]
        for c in range(PAIR // CHUNK):
            rs = slice(c * CHUNK, (c + 1) * CHUNK)
            kuw = [_dot_tn(k_dec[hh][rs], uwb[hh][rs]) for hh in heads]
            sb = [st.astype(BF16) for st in state]
            for hh in heads:
                outs[hh].append(a_uw[hh][rs, :HEAD_DIM] + _dot(q_eff[hh][rs], sb[hh]))
            state = [state[hh] * jnp.exp(gc[hh][(c + 1) * CHUNK - 1:(c + 1) * CHUNK, :]) + kuw[hh][:, :HEAD_DIM]
                     - _dot(kuw[hh][:, HEAD_DIM:].astype(BF16), sb[hh]) for hh in heads]
        for hh in heads:
            s_ref[hh] = state[hh]
            o = jnp.concatenate(outs[hh], axis=0)
            o_ref[0, rows, hh * HEAD_DIM:(hh + 1) * HEAD_DIM] = _rms_rows(o, on_ref[...]).astype(o_ref.dtype)
        return 0

    lax.fori_loop(0, tm // PAIR, pair_body, 0)


def _gdn_call(gqkv, gb, out_norm):
    bsz, t_len, _ = gqkv.shape
    tm = PROJ_ROWS
    row = lambda w: pl.BlockSpec((1, tm, w), lambda b, t: (b, t, 0))
    return pl.pallas_call(
        _gdn_kernel,
        grid=(bsz, t_len // tm),
        in_specs=[row(3 * BRANCH_WIDTH), row(LANES), pl.BlockSpec(out_norm.shape, lambda b, t: (0, 0))],
        out_specs=row(BRANCH_WIDTH),
        out_shape=jax.ShapeDtypeStruct((bsz, t_len, BRANCH_WIDTH), BF16),
        scratch_shapes=[pltpu.VMEM((N_HEADS, HEAD_DIM, HEAD_DIM), F32)],
        compiler_params=_compiler_params(("arbitrary", "arbitrary")),
        name="gated_deltanet",
    )(gqkv, gb, out_norm)


def _hgrn_kernel(hq_ref, hf_ref, hi_ref, lb_ref, on_ref, o_ref, s_ref, diag_ref):
    t = pl.program_id(1)
    tm = hq_ref.shape[1]

    @pl.when(t == 0)
    def _():
        s_ref[...] = jnp.zeros_like(s_ref)

    ii, jj = _chunk_consts()
    incl = (ii >= jj).astype(BF16)
    levels = []
    half = CHUNK // 2
    while half >= HG_SUB:
        row_upper = (ii % (2 * half)) >= half
        same_group = (ii // (2 * half)) == (jj // (2 * half))
        col_lower = (jj % (2 * half)) < half
        levels.append((half, row_upper & same_group & col_lower))
        half //= 2
    sub_j = lax.broadcasted_iota(jnp.int32, (HG_SUB, 1), 0)

    def chunk_body(c, _):
        rows = pl.ds(pl.multiple_of(c * CHUNK, CHUNK), CHUNK)
        heads = range(N_HEADS)
        hs = lambda hh: slice(hh * HEAD_DIM, (hh + 1) * HEAD_DIM)
        lbh = [lb_ref[:, hs(hh)] for hh in heads]
        sig = [_sigmoid(hf_ref[0, rows, hs(hh)]) for hh in heads]
        k = [(1.0 - lbh[hh]) * (1.0 - sig[hh]) for hh in heads]
        g_cum = [_dot_01(incl, jnp.log(lbh[hh] + (1.0 - lbh[hh]) * sig[hh])) for hh in heads]
        q = [hq_ref[0, rows, hs(hh)].astype(F32) for hh in heads]
        vb = [hi_ref[0, rows, hs(hh)] for hh in heads]
        state_t = [s_ref[hh] for hh in heads]
        o = [_dot_nt((q[hh] * jnp.exp(g_cum[hh])).astype(BF16), state_t[hh].astype(BF16)) for hh in heads]

        a_off = [jnp.zeros((CHUNK, CHUNK), F32) for _ in heads]
        for half, lvl_mask in levels:
            for hh in heads:
                ref_rows = jnp.concatenate(
                    [jnp.broadcast_to(g_cum[hh][g0 + half:g0 + half + 1, :], (2 * half, HEAD_DIM))
                     for g0 in range(0, CHUNK, 2 * half)], axis=0)
                q_s = q[hh] * jnp.exp(jnp.minimum(g_cum[hh] - ref_rows, 0.0))
                k_s = k[hh] * jnp.exp(jnp.minimum(ref_rows - g_cum[hh], 0.0))
                a_off[hh] = a_off[hh] + jnp.where(lvl_mask, _dot_nt(q_s.astype(BF16), k_s.astype(BF16)), 0.0)
        o = [o[hh] + _dot(a_off[hh].astype(BF16), vb[hh]) for hh in heads]

        for hh in heads:
            v = vb[hh].astype(F32)
            for i in range(CHUNK):
                b0 = (i // HG_SUB) * HG_SUB
                dec = jnp.exp(jnp.minimum(g_cum[hh][i:i + 1, :] - g_cum[hh][b0:b0 + HG_SUB, :], 0.0))
                a_col = jnp.sum(dec * k[hh][b0:b0 + HG_SUB, :] * q[hh][i:i + 1, :], axis=-1, keepdims=True)
                a_col = jnp.where(sub_j <= (i - b0), a_col, 0.0)
                diag_ref[hh, i:i + 1, :] = jnp.sum(a_col * v[b0:b0 + HG_SUB, :], axis=0, keepdims=True)

        for hh in heads:
            g_end = g_cum[hh][CHUNK - 1:CHUNK, :]
            k_dec = k[hh] * jnp.exp(g_end - g_cum[hh])
            s_ref[hh] = state_t[hh] * jnp.exp(g_end) + _dot_tn(vb[hh], k_dec.astype(BF16))
            o_ref[0, rows, hs(hh)] = _rms_rows(o[hh] + diag_ref[hh], on_ref[...]).astype(o_ref.dtype)
        return 0

    lax.fori_loop(0, tm // CHUNK, chunk_body, 0)


def _hgrn_call(hq, hf, hi, lower_bound, out_norm):
    bsz, t_len, bw = hq.shape
    tm = PROJ_ROWS
    row = pl.BlockSpec((1, tm, bw), lambda b, t: (b, t, 0))
    full = lambda a: pl.BlockSpec(a.shape, lambda b, t: (0, 0))
    return pl.pallas_call(
        _hgrn_kernel,
        grid=(bsz, t_len // tm),
        in_specs=[row, row, row, full(lower_bound), full(out_norm)],
        out_specs=row,
        out_shape=jax.ShapeDtypeStruct((bsz, t_len, bw), BF16),
        scratch_shapes=[pltpu.VMEM((N_HEADS, HEAD_DIM, HEAD_DIM), F32), pltpu.VMEM((N_HEADS, CHUNK, HEAD_DIM), F32)],
        compiler_params=_compiler_params(("arbitrary", "arbitrary")),
        name="hgrn2",
    )(hq, hf, hi, lower_bound, out_norm)


def _merge_kernel(h_ref, osb_ref, ogd_ref, ohg_ref, nw_ref, wz_ref, wmix_ref, wb_ref, wo_ref, out_ref):
    x = h_ref[...]
    xb = _rms_rows(x, nw_ref[...]).astype(BF16)
    y = None
    for b, o_ref in enumerate((osb_ref, ogd_ref, ohg_ref)):
        zg = _dot(xb, wz_ref[:, b * BRANCH_WIDTH:(b + 1) * BRANCH_WIDTH])
        gated = (o_ref[...].astype(F32) * _silu(zg)).astype(BF16)
        mix = _sigmoid(_dot(xb, wmix_ref[:, b * D_MODEL:(b + 1) * D_MODEL]))
        term = mix * _dot(gated, wb_ref[b])
        y = term if y is None else y + term
    out_ref[...] = x + _dot(y.astype(BF16), wo_ref[...])


def _merge_call(h2, osb, ogd, ohg, lw):
    rows, d = h2.shape
    tm = MERGE_ROWS
    assert rows % tm == 0
    row = lambda w: pl.BlockSpec((tm, w), lambda i: (i, 0))
    full = lambda a: pl.BlockSpec(a.shape, lambda i: (0,) * a.ndim)
    weights = (lw["norm_w"], lw["w_z"], lw["w_mix"], lw["w_branch"], lw["w_out"])
    return pl.pallas_call(
        _merge_kernel,
        grid=(rows // tm,),
        in_specs=[row(d), row(BRANCH_WIDTH), row(BRANCH_WIDTH), row(BRANCH_WIDTH)] + [full(w) for w in weights],
        out_specs=row(d),
        out_shape=jax.ShapeDtypeStruct((rows, d), F32),
        compiler_params=_compiler_params(("arbitrary",)),
        name="merge",
    )(h2, osb, ogd, ohg, *weights)


def _layer_weights(layer, norm_w, w_in, sb_q_norm, sb_k_norm, gdn_conv_w, gdn_a_log, gdn_dt_bias, w_branch, w_out):
    bw = BRANCH_WIDTH
    w = w_in[layer]
    o_gqkv = 4 * bw
    o_gz = o_gqkv + 3 * bw
    o_gb = o_gz + bw
    o_hq = o_gb + 2 * N_HEADS
    o_hz = o_hq + 3 * bw
    o_mix = o_hz + bw
    w_ba = jnp.zeros((D_MODEL, LANES), F32).at[:, :2 * N_HEADS].set(w[:, o_gb:o_gb + 2 * N_HEADS])
    pad = LANES - 2 * N_HEADS
    neg_a = jnp.concatenate([jnp.zeros((N_HEADS,), F32), -jnp.exp(gdn_a_log[layer].astype(F32)), jnp.zeros((pad,), F32)])
    dt_b = jnp.concatenate([jnp.zeros((N_HEADS,), F32), gdn_dt_bias[layer].astype(F32), jnp.zeros((pad,), F32)])
    return {
        "norm_w": norm_w[layer].astype(F32)[None, :],
        "w_qv_t": jnp.concatenate([w[:, 0:bw], w[:, 2 * bw:3 * bw]], axis=1).T.astype(BF16),
        "w_k": w[:, bw:2 * bw].astype(BF16),
        "w_gqkv": w[:, o_gqkv:o_gqkv + 3 * bw].astype(BF16),
        "w_ba": w_ba.astype(BF16),
        "w_hg": w[:, o_hq:o_hq + 3 * bw].astype(BF16),
        "sb_qn": (sb_q_norm[layer].astype(F32) * (HEAD_DIM ** -0.5 * LOG2E))[:, None],
        "sb_kn": sb_k_norm[layer].astype(F32)[None, :],
        "conv_w": gdn_conv_w[layer].astype(F32),
        "g_coef": jnp.stack([neg_a, dt_b]),
        "w_z": jnp.concatenate([w[:, 3 * bw:4 * bw], w[:, o_gz:o_gz + bw], w[:, o_hz:o_hz + bw]], axis=1).astype(BF16),
        "w_mix": w[:, o_mix:].astype(BF16),
        "w_branch": w_branch[layer].astype(BF16),
        "w_out": w_out[layer].astype(BF16),
    }


def kernel(x, meta_tokens, norm_w, w_in, sb_q_norm, sb_k_norm, gdn_conv_w, gdn_a_log, gdn_dt_bias, gdn_out_norm,
           hgrn_lb_logits, hgrn_out_norm, w_branch, w_out):
    bsz, _, d = x.shape
    depth = w_in.shape[0]
    h = jnp.concatenate([
        jnp.zeros((bsz, PAD_FRONT, d), x.dtype),
        jnp.broadcast_to(meta_tokens.astype(x.dtype)[None], (bsz, N_META, d)),
        x], axis=1)
    t_len = h.shape[1]
    p = jax.nn.softmax(hgrn_lb_logits.astype(F32), axis=0)
    lower_bounds = jnp.cumsum(p, axis=0) - p[0:1]
    for layer in range(depth):
        lw = _layer_weights(layer, norm_w, w_in, sb_q_norm, sb_k_norm, gdn_conv_w, gdn_a_log, gdn_dt_bias,
                            w_branch, w_out)
        qt, k, vt, gqkv, gb, hq, hf, hi = _proj_call(h, lw)
        o_sb = _sb_call(qt, k, vt)
        o_gd = _gdn_call(gqkv, gb, gdn_out_norm[layer].astype(F32)[None, :])
        o_hg = _hgrn_call(hq, hf, hi, lower_bounds[layer][None, :], hgrn_out_norm[layer].astype(F32)[None, :])
        flat = lambda a: a.reshape(bsz * t_len, a.shape[-1])
        h = _merge_call(flat(h), flat(o_sb), flat(o_gd), flat(o_hg), lw).reshape(bsz, t_len, d)
    return h[:, FRONT:]
```

```python
import functools

import jax
import jax.numpy as jnp
from jax import lax
from jax.experimental import pallas as pl
from jax.experimental.pallas import tpu as pltpu

F32 = jnp.float32
BF16 = jnp.bfloat16

D_MODEL = 1024
BRANCH_WIDTH = D_MODEL // 2
HEAD_DIM = 128
N_HEADS = BRANCH_WIDTH // HEAD_DIM
N_BRANCHES = 3
CONV_WIDTH = 4
CHUNK = 64
N_META = 16
FRONT = 128
PAD_FRONT = FRONT - N_META
NORM_EPS = 1e-6

LANES = 128
SUBLANES = 8
VMEM_LIMIT_BYTES = 56 * 1024 * 1024

PROJ_ROWS = 640
MERGE_ROWS = 512
SB_TILE = 256
SB_SUB = 128
SB_HEADS = 2
PAIR = 2 * CHUNK
HG_SUB = 8
LOG2E = 1.4426950408889634


def _dot(a, b):
    return jnp.dot(a, b, preferred_element_type=F32)


def _dot_nt(a, b):
    return lax.dot_general(a, b, (((1,), (1,)), ((), ())), preferred_element_type=F32)


def _dot_tn(a, b):
    return lax.dot_general(a, b, (((0,), (0,)), ((), ())), preferred_element_type=F32)


def _sigmoid(x):
    return 1.0 / (1.0 + jnp.exp(-x))


def _silu(x):
    return x * _sigmoid(x)


def _rms_rows(x, w):
    return x * lax.rsqrt(jnp.mean(x * x, axis=-1, keepdims=True) + NORM_EPS) * w


def _compiler_params(semantics):
    return pltpu.CompilerParams(dimension_semantics=semantics, vmem_limit_bytes=VMEM_LIMIT_BYTES)


def _proj_kernel(h_ref, nw_ref, wqv_ref, wk_ref, wg_ref, wba_ref, whg_ref, qn_ref, kn_ref, conv_ref, gco_ref,
                 qt_ref, k_ref, vt_ref, gqkv_ref, gb_ref, hq_ref, hf_ref, hi_ref, conv_buf):
    t = pl.program_id(1)
    tm = h_ref.shape[1]
    x = h_ref[0]
    xb = _rms_rows(x, nw_ref[...]).astype(BF16)
    pos = t * tm + lax.broadcasted_iota(jnp.int32, (tm, 1), 0)
    vmask = (pos >= PAD_FRONT).astype(F32)

    qv = _dot_nt(wqv_ref[...], xb)
    for hh in range(N_HEADS):
        qh = qv[hh * HEAD_DIM:(hh + 1) * HEAD_DIM]
        ms = jnp.mean(qh * qh, axis=0, keepdims=True)
        qt_ref[0, hh * HEAD_DIM:(hh + 1) * HEAD_DIM, :] = (qh * lax.rsqrt(ms + NORM_EPS) * qn_ref[...]).astype(BF16)
    vt_ref[0] = qv[BRANCH_WIDTH:].astype(BF16)

    kk = _dot(xb, wk_ref[...])
    for hh in range(N_HEADS):
        sl = slice(hh * HEAD_DIM, (hh + 1) * HEAD_DIM)
        k_ref[0, :, sl] = _rms_rows(kk[:, sl], kn_ref[...]).astype(BF16)

    @pl.when(t == 0)
    def _():
        conv_buf[0:SUBLANES, :] = jnp.zeros((SUBLANES, 3 * BRANCH_WIDTH), F32)

    conv_buf[SUBLANES:SUBLANES + tm, :] = _dot(xb, wg_ref[...])
    first = SUBLANES - (CONV_WIDTH - 1)
    y = conv_buf[first:first + tm, :] * conv_ref[0:1, :]
    for i in range(1, CONV_WIDTH):
        y = y + conv_buf[first + i:first + i + tm, :] * conv_ref[i:i + 1, :]
    conv_buf[0:SUBLANES, :] = conv_buf[tm:tm + SUBLANES, :]
    y = _silu(y)
    for j in range(2 * N_HEADS):
        sl = slice(j * HEAD_DIM, (j + 1) * HEAD_DIM)
        yh = y[:, sl]
        yh = yh * lax.rsqrt(jnp.sum(yh * yh, axis=-1, keepdims=True) + NORM_EPS)
        if j < N_HEADS:
            yh = yh * (HEAD_DIM ** -0.5)
        gqkv_ref[0, :, sl] = yh.astype(BF16)
    gqkv_ref[0, :, 2 * BRANCH_WIDTH:] = y[:, 2 * BRANCH_WIDTH:].astype(BF16)

    ba = _dot(xb, wba_ref[...])
    sp_in = ba + gco_ref[1:2, :]
    softplus = jnp.maximum(sp_in, 0.0) + jnp.log(1.0 + jnp.exp(-jnp.abs(sp_in)))
    lane = lax.broadcasted_iota(jnp.int32, ba.shape, 1)
    gb_ref[0] = jnp.where(lane < N_HEADS, _sigmoid(ba) * vmask, gco_ref[0:1, :] * softplus)

    hg = _dot(xb, whg_ref[...])
    hq_ref[0] = _silu(hg[:, :BRANCH_WIDTH]).astype(BF16)
    hf_ref[0] = hg[:, BRANCH_WIDTH:2 * BRANCH_WIDTH]
    hi_ref[0] = (hg[:, 2 * BRANCH_WIDTH:] * vmask).astype(BF16)


def _proj_call(h, lw):
    bsz, t_len, d = h.shape
    tm = PROJ_ROWS
    assert t_len % tm == 0
    full = lambda a: pl.BlockSpec(a.shape, lambda b, t: (0,) * a.ndim)
    row = lambda w: pl.BlockSpec((1, tm, w), lambda b, t: (b, t, 0))
    col = lambda w: pl.BlockSpec((1, w, tm), lambda b, t: (b, 0, t))
    weights = (lw["norm_w"], lw["w_qv_t"], lw["w_k"], lw["w_gqkv"], lw["w_ba"], lw["w_hg"], lw["sb_qn"], lw["sb_kn"],
               lw["conv_w"], lw["g_coef"])
    bw = BRANCH_WIDTH
    out_shape = (
        jax.ShapeDtypeStruct((bsz, bw, t_len), BF16),
        jax.ShapeDtypeStruct((bsz, t_len, bw), BF16),
        jax.ShapeDtypeStruct((bsz, bw, t_len), BF16),
        jax.ShapeDtypeStruct((bsz, t_len, 3 * bw), BF16),
        jax.ShapeDtypeStruct((bsz, t_len, LANES), F32),
        jax.ShapeDtypeStruct((bsz, t_len, bw), BF16),
        jax.ShapeDtypeStruct((bsz, t_len, bw), F32),
        jax.ShapeDtypeStruct((bsz, t_len, bw), BF16),
    )
    out_specs = (col(bw), row(bw), col(bw), row(3 * bw), row(LANES), row(bw), row(bw), row(bw))
    return pl.pallas_call(
        _proj_kernel,
        grid=(bsz, t_len // tm),
        in_specs=[row(d)] + [full(w) for w in weights],
        out_specs=out_specs,
        out_shape=out_shape,
        scratch_shapes=[pltpu.VMEM((tm + SUBLANES, 3 * bw), F32)],
        compiler_params=_compiler_params(("arbitrary", "arbitrary")),
        name="proj",
    )(h, *weights)


def _sb_kernel(qt_ref, k_ref, vt_ref, o_ref, acc_ref, carry_ref, z_buf, attn_buf):
    t_len = k_ref.shape[1]
    n_tiles = (t_len - FRONT) // SB_TILE
    ii = lax.broadcasted_iota(jnp.int32, (SB_SUB, SB_SUB), 0)
    jj = lax.broadcasted_iota(jnp.int32, (SB_SUB, SB_SUB), 1)
    later = (jj > ii).astype(BF16)
    heads = range(SB_HEADS)
    hs = lambda hd: slice(hd * HEAD_DIM, (hd + 1) * HEAD_DIM)
    pair = [SB_TILE, SB_TILE]

    def visit_order(widths):
        return [(bi, hd) for bi in reversed(range(len(widths))) for hd in heads]

    def scores(q0, nq, k_lo, widths):
        starts = [sum(widths[:i]) for i in range(len(widths))]
        return {(bi, hd): _dot(k_ref[0, pl.ds(k_lo + starts[bi], widths[bi]), hs(hd)], qt_ref[0, hs(hd), pl.ds(q0, nq)])
                for bi, hd in visit_order(widths)}

    def weights(z, q0, nq, k_lo, widths, causal, front):
        starts = [sum(widths[:i]) for i in range(len(widths))]
        order = visit_order(widths)
        mask = {}
        for bi in range(len(widths)):
            mask[bi] = None
            inert = front and k_lo + starts[bi] < PAD_FRONT
            if causal or inert:
                s_idx = k_lo + starts[bi] + lax.broadcasted_iota(jnp.int32, (widths[bi], nq), 0)
                if causal:
                    mask[bi] = s_idx < q0 + lax.broadcasted_iota(jnp.int32, (widths[bi], nq), 1)
                if inert:
                    valid = s_idx >= PAD_FRONT
                    mask[bi] = valid if mask[bi] is None else mask[bi] & valid
        log_beta, keep = {}, {}
        for key in order:
            zz = z[key]
            nz = -zz
            log_keep = jnp.minimum(nz, 0.0) - jnp.log(1.0 + jnp.exp2(jnp.minimum(zz, nz))) * LOG2E
            log_beta[key] = zz + log_keep
            if mask[key[0]] is not None:
                log_keep = jnp.where(mask[key[0]], log_keep, 0.0)
            keep[key] = log_keep.astype(BF16)
        passed = {}
        for bi, hd in order:
            for s in reversed(range(0, widths[bi], SB_SUB)):
                passed[bi, hd, s] = _dot(later, keep[bi, hd][s:s + SB_SUB])
        out = []
        for hd in heads:
            carry = carry_ref[hd, :, :nq]
            parts = []
            for bi in reversed(range(len(widths))):
                for s in reversed(range(0, widths[bi], SB_SUB)):
                    p = passed[bi, hd, s]
                    attn = jnp.exp2(log_beta[bi, hd][s:s + SB_SUB] + p + carry)
                    if mask[bi] is not None:
                        attn = jnp.where(mask[bi][s:s + SB_SUB], attn, 0.0)
                    parts.insert(0, attn.astype(BF16))
                    carry = carry + p[0:1, :] + keep[bi, hd][s:s + 1, :].astype(F32)
            carry_ref[hd, :, :nq] = carry
            out.append(parts[0] if len(parts) == 1 else jnp.concatenate(parts, axis=0))
        return out

    def accumulate(attn, nq, k_lo, total):
        for hd in heads:
            acc_ref[hd, :, :nq] += _dot(vt_ref[0, hs(hd), pl.ds(k_lo, total)], attn[hd])

    def visit(q0, nq, k_lo, widths, causal, front):
        z = scores(q0, nq, k_lo, widths)
        accumulate(weights(z, q0, nq, k_lo, widths, causal, front), nq, k_lo, sum(widths))

    def start():
        acc_ref[...] = jnp.zeros_like(acc_ref)
        carry_ref[...] = jnp.zeros_like(carry_ref)

    def finish(q0, nq):
        for hd in heads:
            o_ref[0, pl.ds(q0, nq), hs(hd)] = acc_ref[hd, :, :nq].T.astype(o_ref.dtype)

    start()
    visit(0, FRONT, 0, [FRONT], True, True)
    finish(0, FRONT)

    def q_body(m, _):
        q0 = pl.multiple_of(FRONT + m * SB_TILE, LANES)
        n_quads = lax.shift_right_logical(m, 2)
        last_pair = 2 * n_quads - 1
        pair_lo = lambda i: pl.multiple_of(q0 - (2 * i + 2) * SB_TILE, LANES)
        start()
        visit(q0, SB_TILE, q0, [SB_TILE], True, False)

        def put_scores(slot, k_lo):
            for key, zz in scores(q0, SB_TILE, k_lo, pair).items():
                z_buf[slot, key[0], key[1]] = zz

        def accumulate_from(slot, k_lo):
            accumulate([attn_buf[slot, hd] for hd in heads], SB_TILE, k_lo, 2 * SB_TILE)

        def pair_step(slot, i):
            accumulate_from(1 - slot, pair_lo(jnp.maximum(i - 1, 0)))
            put_scores(1 - slot, pair_lo(jnp.minimum(i + 1, last_pair)))
            z = {key: z_buf[slot, key[0], key[1]] for key in visit_order(pair)}
            attn = weights(z, q0, SB_TILE, pair_lo(i), pair, False, False)
            for hd in heads:
                attn_buf[slot, hd] = attn[hd]

        @pl.when(n_quads > 0)
        def _():
            attn_buf[1] = jnp.zeros_like(attn_buf[1])
            put_scores(0, pair_lo(0))

            def quad_step(j, _):
                pair_step(0, 2 * j)
                pair_step(1, 2 * j + 1)
                return 0

            lax.fori_loop(0, n_quads, quad_step, 0)
            accumulate_from(1, pair_lo(last_pair))

        for left in range(min(4, n_tiles)):
            @pl.when((m & 3) == left)
            def _():
                visit(q0, SB_TILE, 0, [FRONT] + [SB_TILE] * left, False, True)

        finish(q0, SB_TILE)
        return 0

    lax.fori_loop(0, n_tiles, q_body, 0)


def _sb_call(qt, k, vt):
    bsz, t_len, bw = k.shape
    assert (t_len - FRONT) % SB_TILE == 0 and N_HEADS % SB_HEADS == 0
    hw = SB_HEADS * HEAD_DIM
    col = pl.BlockSpec((1, hw, t_len), lambda b, h: (b, h, 0))
    row = pl.BlockSpec((1, t_len, hw), lambda b, h: (b, 0, h))
    return pl.pallas_call(
        _sb_kernel,
        grid=(bsz, N_HEADS // SB_HEADS),
        in_specs=[col, row, col],
        out_specs=row,
        out_shape=jax.ShapeDtypeStruct((bsz, t_len, bw), BF16),
        scratch_shapes=[pltpu.VMEM((SB_HEADS, HEAD_DIM, SB_TILE), F32), pltpu.VMEM((SB_HEADS, 1, SB_TILE), F32),
                        pltpu.VMEM((2, 2, SB_HEADS, SB_TILE, SB_TILE), F32),
                        pltpu.VMEM((2, SB_HEADS, 2 * SB_TILE, SB_TILE), BF16)],
        compiler_params=_compiler_params(("arbitrary", "arbitrary")),
        name="stick_breaking",
    )(qt, k, vt)


def _chunk_consts():
    ii = lax.broadcasted_iota(jnp.int32, (CHUNK, CHUNK), 0)
    jj = lax.broadcasted_iota(jnp.int32, (CHUNK, CHUNK), 1)
    return ii, jj


def _split3(x):
    hi = x.astype(BF16)
    r = x - hi.astype(F32)
    mid = r.astype(BF16)
    return hi, mid, (r - mid.astype(F32)).astype(BF16)


def _dot_01(tri, x):
    hi, mid, lo = _split3(x)
    return _dot(tri, hi) + _dot(tri, mid) + _dot(tri, lo)


def _pair_masks():
    ii = lax.broadcasted_iota(jnp.int32, (PAIR, PAIR), 0)
    jj = lax.broadcasted_iota(jnp.int32, (PAIR, PAIR), 1)
    same = (ii // CHUNK) == (jj // CHUNK)
    return same & (ii >= jj), same & (ii > jj)


def _gdn_kernel(gqkv_ref, gb_ref, on_ref, o_ref, s_ref):
    t = pl.program_id(1)
    tm = gqkv_ref.shape[1]
    bw = BRANCH_WIDTH

    @pl.when(t == 0)
    def _():
        s_ref[...] = jnp.zeros_like(s_ref)

    causal, strict = _pair_masks()
    incl = causal.astype(BF16)
    second = lax.broadcasted_iota(jnp.int32, (PAIR, 1), 0) >= CHUNK

    def pair_body(d, _):
        rows = pl.ds(pl.multiple_of(d * PAIR, PAIR), PAIR)
        gbc = gb_ref[0, rows, :]
        g_cum = _dot_01(incl, gbc)
        g_cum_t = g_cum.T
        qs = [gqkv_ref[0, rows, hh * HEAD_DIM:(hh + 1) * HEAD_DIM] for hh in range(N_HEADS)]
        ks = [gqkv_ref[0, rows, bw + hh * HEAD_DIM:bw + (hh + 1) * HEAD_DIM] for hh in range(N_HEADS)]
        vs = [gqkv_ref[0, rows, 2 * bw + hh * HEAD_DIM:2 * bw + (hh + 1) * HEAD_DIM] for hh in range(N_HEADS)]
        heads = range(N_HEADS)
        kf = [k.astype(F32) for k in ks]
        beta = [gbc[:, hh:hh + 1] for hh in heads]
        gc = [g_cum[:, N_HEADS + hh:N_HEADS + hh + 1] for hh in heads]
        decay = [jnp.exp(jnp.minimum(gc[hh] - g_cum_t[N_HEADS + hh:N_HEADS + hh + 1, :], 0.0))
                 for hh in heads]
        kb = [kf[hh] * beta[hh] for hh in heads]
        m = [jnp.where(strict, _dot_nt(kb[hh].astype(BF16), ks[hh]) * decay[hh], 0.0) for hh in heads]
        a_qk = [jnp.where(causal, _dot_nt(qs[hh], ks[hh]) * decay[hh], 0.0).astype(BF16) for hh in heads]
        e = [-mm for mm in m]
        pb = [mm.astype(BF16) for mm in m]
        for _ in range(5):
            p = [_dot(pb[hh], pb[hh]) for hh in heads]
            pb = [pp.astype(BF16) for pp in p]
            e = [e[hh] + p[hh] + _dot(e[hh].astype(BF16), pb[hh]) for hh in heads]
        eg = [jnp.exp(g) for g in gc]
        rhs = [jnp.concatenate([vs[hh].astype(F32) * beta[hh], kb[hh] * eg[hh]], axis=1) for hh in heads]
        uwb = [(rhs[hh] + _dot(e[hh].astype(BF16), rhs[hh].astype(BF16))).astype(BF16) for hh in heads]
        a_uw = [_dot(a_qk[hh], uwb[hh]) for hh in heads]
        q_eff = [(qs[hh].astype(F32) * eg[hh] - a_uw[hh][:, HEAD_DIM:]).astype(BF16) for hh in heads]
        k_dec = []
        for hh in heads:
            g_last = jnp.where(second, gc[hh][PAIR - 1:PAIR, :], gc[hh][CHUNK - 1:CHUNK, :])
            k_dec.append((kf[hh] * jnp.exp(g_last - gc[hh])).astype(BF16))
        state = [s_ref[hh] for hh in heads]
        outs = [[] for _ in heads]
        for c in range(PAIR // CHUNK):
            rs = slice(c * CHUNK, (c + 1) * CHUNK)
            kuw = [_dot_tn(k_dec[hh][rs], uwb[hh][rs]) for hh in heads]
            sb = [st.astype(BF16) for st in state]
            for hh in heads:
                outs[hh].append(a_uw[hh][rs, :HEAD_DIM] + _dot(q_eff[hh][rs], sb[hh]))
            state = [state[hh] * jnp.exp(gc[hh][(c + 1) * CHUNK - 1:(c + 1) * CHUNK, :]) + kuw[hh][:, :HEAD_DIM]
                     - _dot(kuw[hh][:, HEAD_DIM:].astype(BF16), sb[hh]) for hh in heads]
        for hh in heads:
            s_ref[hh] = state[hh]
            o = jnp.concatenate(outs[hh], axis=0)
            o_ref[0, rows, hh * HEAD_DIM:(hh + 1) * HEAD_DIM] = _rms_rows(o, on_ref[...]).astype(o_ref.dtype)
        return 0

    lax.fori_loop(0, tm // PAIR, pair_body, 0)


def _gdn_call(gqkv, gb, out_norm):
    bsz, t_len, _ = gqkv.shape
    tm = PROJ_ROWS
    row = lambda w: pl.BlockSpec((1, tm, w), lambda b, t: (b, t, 0))
    return pl.pallas_call(
        _gdn_kernel,
        grid=(bsz, t_len // tm),
        in_specs=[row(3 * BRANCH_WIDTH), row(LANES), pl.BlockSpec(out_norm.shape, lambda b, t: (0, 0))],
        out_specs=row(BRANCH_WIDTH),
        out_shape=jax.ShapeDtypeStruct((bsz, t_len, BRANCH_WIDTH), BF16),
        scratch_shapes=[pltpu.VMEM((N_HEADS, HEAD_DIM, HEAD_DIM), F32)],
        compiler_params=_compiler_params(("arbitrary", "arbitrary")),
        name="gated_deltanet",
    )(gqkv, gb, out_norm)


def _hgrn_kernel(hq_ref, hf_ref, hi_ref, lb_ref, on_ref, o_ref, s_ref):
    t = pl.program_id(1)
    tm = hq_ref.shape[1]

    @pl.when(t == 0)
    def _():
        s_ref[...] = jnp.zeros_like(s_ref)

    ii, jj = _chunk_consts()
    incl = (ii >= jj).astype(BF16)
    levels = []
    half = CHUNK // 2
    while half >= HG_SUB:
        row_upper = (ii % (2 * half)) >= half
        same_group = (ii // (2 * half)) == (jj // (2 * half))
        col_lower = (jj % (2 * half)) < half
        levels.append((half, row_upper & same_group & col_lower))
        half //= 2
    sub_lane = lax.broadcasted_iota(jnp.int32, (HG_SUB, CHUNK), 1)

    def chunk_body(c, _):
        rows = pl.ds(pl.multiple_of(c * CHUNK, CHUNK), CHUNK)
        heads = range(N_HEADS)
        hs = lambda hh: slice(hh * HEAD_DIM, (hh + 1) * HEAD_DIM)
        lbh = [lb_ref[:, hs(hh)] for hh in heads]
        sig = [_sigmoid(hf_ref[0, rows, hs(hh)]) for hh in heads]
        k = [(1.0 - lbh[hh]) * (1.0 - sig[hh]) for hh in heads]
        g_cum = [_dot_01(incl, jnp.log(lbh[hh] + (1.0 - lbh[hh]) * sig[hh])) for hh in heads]
        q = [hq_ref[0, rows, hs(hh)].astype(F32) for hh in heads]
        vb = [hi_ref[0, rows, hs(hh)] for hh in heads]
        state_t = [s_ref[hh] for hh in heads]
        o = [_dot_nt((q[hh] * jnp.exp(g_cum[hh])).astype(BF16), state_t[hh].astype(BF16)) for hh in heads]

        a_off = [jnp.zeros((CHUNK, CHUNK), F32) for _ in heads]
        for half, lvl_mask in levels:
            for hh in heads:
                ref_rows = jnp.concatenate(
                    [jnp.broadcast_to(g_cum[hh][g0 + half:g0 + half + 1, :], (2 * half, HEAD_DIM))
                     for g0 in range(0, CHUNK, 2 * half)], axis=0)
                q_s = q[hh] * jnp.exp(jnp.minimum(g_cum[hh] - ref_rows, 0.0))
                k_s = k[hh] * jnp.exp(jnp.minimum(ref_rows - g_cum[hh], 0.0))
                a_off[hh] = a_off[hh] + jnp.where(lvl_mask, _dot_nt(q_s.astype(BF16), k_s.astype(BF16)), 0.0)
        for hh in heads:
            blocks = []
            for b0 in range(0, CHUNK, HG_SUB):
                g_blk = g_cum[hh][b0:b0 + HG_SUB, :]
                q_blk = q[hh][b0:b0 + HG_SUB, :]
                a_blk = jnp.zeros((HG_SUB, CHUNK), F32)
                for j in range(b0, b0 + HG_SUB):
                    dec = jnp.exp(jnp.minimum(g_blk - g_cum[hh][j:j + 1, :], 0.0))
                    col = jnp.sum(dec * (q_blk * k[hh][j:j + 1, :]), axis=-1, keepdims=True)
                    a_blk = jnp.where(sub_lane == j, col, a_blk)
                blocks.append(a_blk)
            a_diag = jnp.where(ii >= jj, jnp.concatenate(blocks, axis=0), 0.0)
            o[hh] = o[hh] + _dot((a_off[hh] + a_diag).astype(BF16), vb[hh])

        for hh in heads:
            g_end = g_cum[hh][CHUNK - 1:CHUNK, :]
            k_dec = k[hh] * jnp.exp(g_end - g_cum[hh])
            s_ref[hh] = state_t[hh] * jnp.exp(g_end) + _dot_tn(vb[hh], k_dec.astype(BF16))
            o_ref[0, rows, hs(hh)] = _rms_rows(o[hh], on_ref[...]).astype(o_ref.dtype)
        return 0

    lax.fori_loop(0, tm // CHUNK, chunk_body, 0)


def _hgrn_call(hq, hf, hi, lower_bound, out_norm):
    bsz, t_len, bw = hq.shape
    tm = PROJ_ROWS
    row = pl.BlockSpec((1, tm, bw), lambda b, t: (b, t, 0))
    full = lambda a: pl.BlockSpec(a.shape, lambda b, t: (0, 0))
    return pl.pallas_call(
        _hgrn_kernel,
        grid=(bsz, t_len // tm),
        in_specs=[row, row, row, full(lower_bound), full(out_norm)],
        out_specs=row,
        out_shape=jax.ShapeDtypeStruct((bsz, t_len, bw), BF16),
        scratch_shapes=[pltpu.VMEM((N_HEADS, HEAD_DIM, HEAD_DIM), F32)],
        compiler_params=_compiler_params(("arbitrary", "arbitrary")),
        name="hgrn2",
    )(hq, hf, hi, lower_bound, out_norm)


def _merge_kernel(h_ref, osb_ref, ogd_ref, ohg_ref, nw_ref, wz_ref, wmix_ref, wb_ref, wo_ref, out_ref):
    x = h_ref[...]
    xb = _rms_rows(x, nw_ref[...]).astype(BF16)
    y = None
    for b, o_ref in enumerate((osb_ref, ogd_ref, ohg_ref)):
        zg = _dot(xb, wz_ref[:, b * BRANCH_WIDTH:(b + 1) * BRANCH_WIDTH])
        gated = (o_ref[...].astype(F32) * _silu(zg)).astype(BF16)
        mix = _sigmoid(_dot(xb, wmix_ref[:, b * D_MODEL:(b + 1) * D_MODEL]))
        term = mix * _dot(gated, wb_ref[b])
        y = term if y is None else y + term
    out_ref[...] = x + _dot(y.astype(BF16), wo_ref[...])


def _merge_call(h2, osb, ogd, ohg, lw):
    rows, d = h2.shape
    tm = MERGE_ROWS
    assert rows % tm == 0
    row = lambda w: pl.BlockSpec((tm, w), lambda i: (i, 0))
    full = lambda a: pl.BlockSpec(a.shape, lambda i: (0,) * a.ndim)
    weights = (lw["norm_w"], lw["w_z"], lw["w_mix"], lw["w_branch"], lw["w_out"])
    return pl.pallas_call(
        _merge_kernel,
        grid=(rows // tm,),
        in_specs=[row(d), row(BRANCH_WIDTH), row(BRANCH_WIDTH), row(BRANCH_WIDTH)] + [full(w) for w in weights],
        out_specs=row(d),
        out_shape=jax.ShapeDtypeStruct((rows, d), F32),
        compiler_params=_compiler_params(("arbitrary",)),
        name="merge",
    )(h2, osb, ogd, ohg, *weights)


def _layer_weights(layer, norm_w, w_in, sb_q_norm, sb_k_norm, gdn_conv_w, gdn_a_log, gdn_dt_bias, w_branch, w_out):
    bw = BRANCH_WIDTH
    w = w_in[layer]
    o_gqkv = 4 * bw
    o_gz = o_gqkv + 3 * bw
    o_gb = o_gz + bw
    o_hq = o_gb + 2 * N_HEADS
    o_hz = o_hq + 3 * bw
    o_mix = o_hz + bw
    w_ba = jnp.zeros((D_MODEL, LANES), F32).at[:, :2 * N_HEADS].set(w[:, o_gb:o_gb + 2 * N_HEADS])
    pad = LANES - 2 * N_HEADS
    neg_a = jnp.concatenate([jnp.zeros((N_HEADS,), F32), -jnp.exp(gdn_a_log[layer].astype(F32)), jnp.zeros((pad,), F32)])
    dt_b = jnp.concatenate([jnp.zeros((N_HEADS,), F32), gdn_dt_bias[layer].astype(F32), jnp.zeros((pad,), F32)])
    return {
        "norm_w": norm_w[layer].astype(F32)[None, :],
        "w_qv_t": jnp.concatenate([w[:, 0:bw], w[:, 2 * bw:3 * bw]], axis=1).T.astype(BF16),
        "w_k": w[:, bw:2 * bw].astype(BF16),
        "w_gqkv": w[:, o_gqkv:o_gqkv + 3 * bw].astype(BF16),
        "w_ba": w_ba.astype(BF16),
        "w_hg": w[:, o_hq:o_hq + 3 * bw].astype(BF16),
        "sb_qn": (sb_q_norm[layer].astype(F32) * (HEAD_DIM ** -0.5 * LOG2E))[:, None],
        "sb_kn": sb_k_norm[layer].astype(F32)[None, :],
        "conv_w": gdn_conv_w[layer].astype(F32),
        "g_coef": jnp.stack([neg_a, dt_b]),
        "w_z": jnp.concatenate([w[:, 3 * bw:4 * bw], w[:, o_gz:o_gz + bw], w[:, o_hz:o_hz + bw]], axis=1).astype(BF16),
        "w_mix": w[:, o_mix:].astype(BF16),
        "w_branch": w_branch[layer].astype(BF16),
        "w_out": w_out[layer].astype(BF16),
    }


def kernel(x, meta_tokens, norm_w, w_in, sb_q_norm, sb_k_norm, gdn_conv_w, gdn_a_log, gdn_dt_bias, gdn_out_norm,
           hgrn_lb_logits, hgrn_out_norm, w_branch, w_out):
    bsz, _, d = x.shape
    depth = w_in.shape[0]
    h = jnp.concatenate([
        jnp.zeros((bsz, PAD_FRONT, d), x.dtype),
        jnp.broadcast_to(meta_tokens.astype(x.dtype)[None], (bsz, N_META, d)),
        x], axis=1)
    t_len = h.shape[1]
    p = jax.nn.softmax(hgrn_lb_logits.astype(F32), axis=0)
    lower_bounds = jnp.cumsum(p, axis=0) - p[0:1]
    for layer in range(depth):
        lw = _layer_weights(layer, norm_w, w_in, sb_q_norm, sb_k_norm, gdn_conv_w, gdn_a_log, gdn_dt_bias,
                            w_branch, w_out)
        qt, k, vt, gqkv, gb, hq, hf, hi = _proj_call(h, lw)
        o_sb = _sb_call(qt, k, vt)
        o_gd = _gdn_call(gqkv, gb, gdn_out_norm[layer].astype(F32)[None, :])
        o_hg = _hgrn_call(hq, hf, hi, lower_bounds[layer][None, :], hgrn_out_norm[layer].astype(F32)[None, :])
        flat = lambda a: a.reshape(bsz * t_len, a.shape[-1])
        h = _merge_call(flat(h), flat(o_sb), flat(o_gd), flat(o_hg), lw).reshape(bsz, t_len, d)
    return h[:, FRONT:]
```

```python
import functools

import jax
import jax.numpy as jnp
from jax import lax
from jax.experimental import pallas as pl
from jax.experimental.pallas import tpu as pltpu

F32 = jnp.float32
BF16 = jnp.bfloat16

D_MODEL = 1024
BRANCH_WIDTH = D_MODEL // 2
HEAD_DIM = 128
N_HEADS = BRANCH_WIDTH // HEAD_DIM
N_BRANCHES = 3
CONV_WIDTH = 4
CHUNK = 64
N_META = 16
FRONT = 128
PAD_FRONT = FRONT - N_META
NORM_EPS = 1e-6

LANES = 128
SUBLANES = 8
VMEM_LIMIT_BYTES = 56 * 1024 * 1024

PROJ_ROWS = 640
MERGE_ROWS = 512
SB_TILE = 256
SB_SUB = 128
SB_HEADS = 2
SB_DEAD = 160.0
PAIR = 2 * CHUNK
GDN_GROUP = 5
HG_GROUP = 2
HG_SUB = 8
LOG2E = 1.4426950408889634


def _dot(a, b):
    return jnp.dot(a, b, preferred_element_type=F32)


def _dot_nt(a, b):
    return lax.dot_general(a, b, (((1,), (1,)), ((), ())), preferred_element_type=F32)


def _dot_tn(a, b):
    return lax.dot_general(a, b, (((0,), (0,)), ((), ())), preferred_element_type=F32)


def _sigmoid(x):
    return 1.0 / (1.0 + jnp.exp(-x))


def _silu(x):
    return x * _sigmoid(x)


def _rms_rows(x, w):
    return x * lax.rsqrt(jnp.mean(x * x, axis=-1, keepdims=True) + NORM_EPS) * w


def _compiler_params(semantics):
    return pltpu.CompilerParams(dimension_semantics=semantics, vmem_limit_bytes=VMEM_LIMIT_BYTES)


def _proj_kernel(h_ref, nw_ref, wqv_ref, wk_ref, wg_ref, wba_ref, whg_ref, qn_ref, kn_ref, conv_ref, gco_ref,
                 qt_ref, k_ref, vt_ref, gqkv_ref, gb_ref, hq_ref, hf_ref, hi_ref, conv_buf):
    t = pl.program_id(1)
    tm = h_ref.shape[1]
    x = h_ref[0]
    xb = _rms_rows(x, nw_ref[...]).astype(BF16)
    pos = t * tm + lax.broadcasted_iota(jnp.int32, (tm, 1), 0)
    vmask = (pos >= PAD_FRONT).astype(F32)

    qv = _dot_nt(wqv_ref[...], xb)
    for hh in range(N_HEADS):
        qh = qv[hh * HEAD_DIM:(hh + 1) * HEAD_DIM]
        ms = jnp.mean(qh * qh, axis=0, keepdims=True)
        qt_ref[0, hh * HEAD_DIM:(hh + 1) * HEAD_DIM, :] = (qh * lax.rsqrt(ms + NORM_EPS) * qn_ref[...]).astype(BF16)
    vt_ref[0] = qv[BRANCH_WIDTH:].astype(BF16)

    kk = _dot(xb, wk_ref[...])
    for hh in range(N_HEADS):
        sl = slice(hh * HEAD_DIM, (hh + 1) * HEAD_DIM)
        k_ref[0, :, sl] = _rms_rows(kk[:, sl], kn_ref[...]).astype(BF16)

    @pl.when(t == 0)
    def _():
        conv_buf[0:SUBLANES, :] = jnp.zeros((SUBLANES, 3 * BRANCH_WIDTH), F32)

    conv_buf[SUBLANES:SUBLANES + tm, :] = _dot(xb, wg_ref[...])
    first = SUBLANES - (CONV_WIDTH - 1)
    y = conv_buf[first:first + tm, :] * conv_ref[0:1, :]
    for i in range(1, CONV_WIDTH):
        y = y + conv_buf[first + i:first + i + tm, :] * conv_ref[i:i + 1, :]
    conv_buf[0:SUBLANES, :] = conv_buf[tm:tm + SUBLANES, :]
    y = _silu(y)
    for j in range(2 * N_HEADS):
        sl = slice(j * HEAD_DIM, (j + 1) * HEAD_DIM)
        yh = y[:, sl]
        yh = yh * lax.rsqrt(jnp.sum(yh * yh, axis=-1, keepdims=True) + NORM_EPS)
        if j < N_HEADS:
            yh = yh * (HEAD_DIM ** -0.5)
        gqkv_ref[0, :, sl] = yh.astype(BF16)
    gqkv_ref[0, :, 2 * BRANCH_WIDTH:] = y[:, 2 * BRANCH_WIDTH:].astype(BF16)

    ba = _dot(xb, wba_ref[...])
    sp_in = ba + gco_ref[1:2, :]
    softplus = jnp.maximum(sp_in, 0.0) + jnp.log(1.0 + jnp.exp(-jnp.abs(sp_in)))
    lane = lax.broadcasted_iota(jnp.int32, ba.shape, 1)
    gb_ref[0] = jnp.where(lane < N_HEADS, _sigmoid(ba) * vmask, gco_ref[0:1, :] * softplus)

    hg = _dot(xb, whg_ref[...])
    hq_ref[0] = _silu(hg[:, :BRANCH_WIDTH]).astype(BF16)
    hf_ref[0] = hg[:, BRANCH_WIDTH:2 * BRANCH_WIDTH]
    hi_ref[0] = (hg[:, 2 * BRANCH_WIDTH:] * vmask).astype(BF16)


def _proj_call(h, lw):
    bsz, t_len, d = h.shape
    tm = PROJ_ROWS
    assert t_len % tm == 0
    full = lambda a: pl.BlockSpec(a.shape, lambda b, t: (0,) * a.ndim)
    row = lambda w: pl.BlockSpec((1, tm, w), lambda b, t: (b, t, 0))
    col = lambda w: pl.BlockSpec((1, w, tm), lambda b, t: (b, 0, t))
    weights = (lw["norm_w"], lw["w_qv_t"], lw["w_k"], lw["w_gqkv"], lw["w_ba"], lw["w_hg"], lw["sb_qn"], lw["sb_kn"],
               lw["conv_w"], lw["g_coef"])
    bw = BRANCH_WIDTH
    out_shape = (
        jax.ShapeDtypeStruct((bsz, bw, t_len), BF16),
        jax.ShapeDtypeStruct((bsz, t_len, bw), BF16),
        jax.ShapeDtypeStruct((bsz, bw, t_len), BF16),
        jax.ShapeDtypeStruct((bsz, t_len, 3 * bw), BF16),
        jax.ShapeDtypeStruct((bsz, t_len, LANES), F32),
        jax.ShapeDtypeStruct((bsz, t_len, bw), BF16),
        jax.ShapeDtypeStruct((bsz, t_len, bw), F32),
        jax.ShapeDtypeStruct((bsz, t_len, bw), BF16),
    )
    out_specs = (col(bw), row(bw), col(bw), row(3 * bw), row(LANES), row(bw), row(bw), row(bw))
    return pl.pallas_call(
        _proj_kernel,
        grid=(bsz, t_len // tm),
        in_specs=[row(d)] + [full(w) for w in weights],
        out_specs=out_specs,
        out_shape=out_shape,
        scratch_shapes=[pltpu.VMEM((tm + SUBLANES, 3 * bw), F32)],
        compiler_params=_compiler_params(("arbitrary", "arbitrary")),
        name="proj",
    )(h, *weights)


def _sb_kernel(qt_ref, k_ref, vt_ref, o_ref, acc_ref, carry_ref, z_buf, attn_buf):
    t_len = k_ref.shape[1]
    n_tiles = (t_len - FRONT) // SB_TILE
    ii = lax.broadcasted_iota(jnp.int32, (SB_SUB, SB_SUB), 0)
    jj = lax.broadcasted_iota(jnp.int32, (SB_SUB, SB_SUB), 1)
    later = (jj > ii).astype(BF16)
    heads = range(SB_HEADS)
    hs = lambda hd: slice(hd * HEAD_DIM, (hd + 1) * HEAD_DIM)
    pair = [SB_TILE, SB_TILE]

    def visit_order(widths):
        return [(bi, hd) for bi in reversed(range(len(widths))) for hd in heads]

    def scores(q0, nq, k_lo, widths):
        starts = [sum(widths[:i]) for i in range(len(widths))]
        return {(bi, hd): _dot(k_ref[0, pl.ds(k_lo + starts[bi], widths[bi]), hs(hd)], qt_ref[0, hs(hd), pl.ds(q0, nq)])
                for bi, hd in visit_order(widths)}

    def weights(z, q0, nq, k_lo, widths, causal, front):
        starts = [sum(widths[:i]) for i in range(len(widths))]
        order = visit_order(widths)
        mask = {}
        for bi in range(len(widths)):
            mask[bi] = None
            inert = front and k_lo + starts[bi] < PAD_FRONT
            if bi in causal or inert:
                s_idx = k_lo + starts[bi] + lax.broadcasted_iota(jnp.int32, (widths[bi], nq), 0)
                if bi in causal:
                    mask[bi] = s_idx < q0 + lax.broadcasted_iota(jnp.int32, (widths[bi], nq), 1)
                if inert:
                    valid = s_idx >= PAD_FRONT
                    mask[bi] = valid if mask[bi] is None else mask[bi] & valid
        log_beta, keep = {}, {}
        for key in order:
            zz = z[key]
            nz = -zz
            log_keep = jnp.minimum(nz, 0.0) - jnp.log(1.0 + jnp.exp2(jnp.minimum(zz, nz))) * LOG2E
            log_beta[key] = zz + log_keep
            if mask[key[0]] is not None:
                log_keep = jnp.where(mask[key[0]], log_keep, 0.0)
            keep[key] = log_keep.astype(BF16)
        passed = {}
        for bi, hd in order:
            for s in reversed(range(0, widths[bi], SB_SUB)):
                passed[bi, hd, s] = _dot(later, keep[bi, hd][s:s + SB_SUB])
        out = []
        for hd in heads:
            carry = carry_ref[hd, :, :nq]
            parts = []
            for bi in reversed(range(len(widths))):
                for s in reversed(range(0, widths[bi], SB_SUB)):
                    p = passed[bi, hd, s]
                    attn = jnp.exp2(log_beta[bi, hd][s:s + SB_SUB] + p + carry)
                    if mask[bi] is not None:
                        attn = jnp.where(mask[bi][s:s + SB_SUB], attn, 0.0)
                    parts.insert(0, attn.astype(BF16))
                    carry = carry + p[0:1, :] + keep[bi, hd][s:s + 1, :].astype(F32)
            carry_ref[hd, :, :nq] = carry
            out.append(parts[0] if len(parts) == 1 else jnp.concatenate(parts, axis=0))
        return out

    def accumulate(attn, nq, k_lo, total):
        for hd in heads:
            acc_ref[hd, :, :nq] += _dot(vt_ref[0, hs(hd), pl.ds(k_lo, total)], attn[hd])

    def visit(q0, nq, k_lo, widths, causal, front):
        z = scores(q0, nq, k_lo, widths)
        accumulate(weights(z, q0, nq, k_lo, widths, causal, front), nq, k_lo, sum(widths))

    def start():
        acc_ref[...] = jnp.zeros_like(acc_ref)
        carry_ref[...] = jnp.zeros_like(carry_ref)

    def finish(q0, nq):
        for hd in heads:
            o_ref[0, pl.ds(q0, nq), hs(hd)] = acc_ref[hd, :, :nq].T.astype(o_ref.dtype)

    def alive():
        return (jnp.max(carry_ref[...]) > -SB_DEAD).astype(jnp.int32)

    start()
    visit(0, FRONT, 0, [FRONT], [0], True)
    finish(0, FRONT)

    def q_body(m, _):
        q0 = pl.multiple_of(FRONT + m * SB_TILE, LANES)
        start()

        @pl.when(m == 0)
        def _():
            visit(q0, SB_TILE, q0, [SB_TILE], [0], False)

        @pl.when(m > 0)
        def _():
            visit(q0, SB_TILE, q0 - SB_TILE, pair, [1], False)

        rest = jnp.maximum(m - 1, 0)
        n_quads = lax.shift_right_logical(rest, 2)
        last_pair = 2 * n_quads - 1
        pair_lo = lambda i: pl.multiple_of(q0 - (2 * i + 3) * SB_TILE, LANES)

        def put_scores(slot, k_lo):
            for key, zz in scores(q0, SB_TILE, k_lo, pair).items():
                z_buf[slot, key[0], key[1]] = zz

        def accumulate_from(slot, k_lo):
            accumulate([attn_buf[slot, hd] for hd in heads], SB_TILE, k_lo, 2 * SB_TILE)

        def pair_step(slot, i):
            accumulate_from(1 - slot, pair_lo(jnp.maximum(i - 1, 0)))
            put_scores(1 - slot, pair_lo(jnp.minimum(i + 1, last_pair)))
            z = {key: z_buf[slot, key[0], key[1]] for key in visit_order(pair)}
            attn = weights(z, q0, SB_TILE, pair_lo(i), pair, [], False)
            for hd in heads:
                attn_buf[slot, hd] = attn[hd]

        @pl.when(alive() == 1)
        def _():
            @pl.when(n_quads > 0)
            def _():
                attn_buf[1] = jnp.zeros_like(attn_buf[1])
                put_scores(0, pair_lo(0))

                def quad_step(state):
                    j, _ = state
                    pair_step(0, 2 * j)
                    pair_step(1, 2 * j + 1)
                    return j + 1, alive()

                done, _ = lax.while_loop(lambda s: (s[0] < n_quads) & (s[1] == 1), quad_step,
                                         (jnp.int32(0), jnp.int32(1)))
                accumulate_from(1, pair_lo(2 * done - 1))

            still_alive = alive()
            for left in range(min(4, n_tiles)):
                @pl.when(((rest & 3) == left) & (still_alive == 1))
                def _():
                    visit(q0, SB_TILE, 0, [FRONT] + [SB_TILE] * left, [], True)

        finish(q0, SB_TILE)
        return 0

    lax.fori_loop(0, n_tiles, q_body, 0)


def _sb_call(qt, k, vt):
    bsz, t_len, bw = k.shape
    assert (t_len - FRONT) % SB_TILE == 0 and N_HEADS % SB_HEADS == 0
    hw = SB_HEADS * HEAD_DIM
    col = pl.BlockSpec((1, hw, t_len), lambda b, h: (b, h, 0))
    row = pl.BlockSpec((1, t_len, hw), lambda b, h: (b, 0, h))
    return pl.pallas_call(
        _sb_kernel,
        grid=(bsz, N_HEADS // SB_HEADS),
        in_specs=[col, row, col],
        out_specs=row,
        out_shape=jax.ShapeDtypeStruct((bsz, t_len, bw), BF16),
        scratch_shapes=[pltpu.VMEM((SB_HEADS, HEAD_DIM, SB_TILE), F32), pltpu.VMEM((SB_HEADS, 1, SB_TILE), F32),
                        pltpu.VMEM((2, 2, SB_HEADS, SB_TILE, SB_TILE), F32),
                        pltpu.VMEM((2, SB_HEADS, 2 * SB_TILE, SB_TILE), BF16)],
        compiler_params=_compiler_params(("arbitrary", "arbitrary")),
        name="stick_breaking",
    )(qt, k, vt)


def _chunk_consts():
    ii = lax.broadcasted_iota(jnp.int32, (CHUNK, CHUNK), 0)
    jj = lax.broadcasted_iota(jnp.int32, (CHUNK, CHUNK), 1)
    return ii, jj


def _split3(x):
    hi = x.astype(BF16)
    r = x - hi.astype(F32)
    mid = r.astype(BF16)
    return hi, mid, (r - mid.astype(F32)).astype(BF16)


def _dot_01(tri, x):
    hi, mid, lo = _split3(x)
    return _dot(tri, hi) + _dot(tri, mid) + _dot(tri, lo)


def _pair_masks():
    ii = lax.broadcasted_iota(jnp.int32, (PAIR, PAIR), 0)
    jj = lax.broadcasted_iota(jnp.int32, (PAIR, PAIR), 1)
    same = (ii // CHUNK) == (jj // CHUNK)
    return same & (ii >= jj), same & (ii > jj)


def _gdn_kernel(gqkv_ref, gb_ref, on_ref, o_ref, s_ref):
    t = pl.program_id(1)
    tm = gqkv_ref.shape[1]
    bw = BRANCH_WIDTH

    @pl.when(t == 0)
    def _():
        s_ref[...] = jnp.zeros_like(s_ref)

    causal, strict = _pair_masks()
    incl = causal.astype(BF16)
    second = lax.broadcasted_iota(jnp.int32, (PAIR, 1), 0) >= CHUNK

    heads = range(N_HEADS)
    halves = range(PAIR // CHUNK)
    half = lambda c: slice(c * CHUNK, (c + 1) * CHUNK)

    def group_body(grp, _):
        pairs = range(GDN_GROUP)
        chains = [(d, hh) for d in pairs for hh in heads]
        rows = [pl.ds(pl.multiple_of((grp * GDN_GROUP + d) * PAIR, PAIR), PAIR) for d in pairs]
        gbc = [gb_ref[0, rows[d], :] for d in pairs]
        g_cum = [_dot_01(incl, gbc[d]) for d in pairs]
        g_cum_t = [g.T for g in g_cum]
        q = {(d, hh): gqkv_ref[0, rows[d], hh * HEAD_DIM:(hh + 1) * HEAD_DIM] for d, hh in chains}
        k = {(d, hh): gqkv_ref[0, rows[d], bw + hh * HEAD_DIM:bw + (hh + 1) * HEAD_DIM] for d, hh in chains}
        v = {(d, hh): gqkv_ref[0, rows[d], 2 * bw + hh * HEAD_DIM:2 * bw + (hh + 1) * HEAD_DIM] for d, hh in chains}
        kf = {c: k[c].astype(F32) for c in chains}
        beta = {(d, hh): gbc[d][:, hh:hh + 1] for d, hh in chains}
        gc = {(d, hh): g_cum[d][:, N_HEADS + hh:N_HEADS + hh + 1] for d, hh in chains}
        decay = {(d, hh): jnp.exp(jnp.minimum(gc[d, hh] - g_cum_t[d][N_HEADS + hh:N_HEADS + hh + 1, :], 0.0))
                 for d, hh in chains}
        kb = {c: kf[c] * beta[c] for c in chains}
        m = {c: jnp.where(strict, _dot_nt(kb[c].astype(BF16), k[c]) * decay[c], 0.0) for c in chains}
        a_qk = {c: jnp.where(causal, _dot_nt(q[c], k[c]) * decay[c], 0.0).astype(BF16) for c in chains}
        e = {c: -m[c] for c in chains}
        pb = {c: m[c].astype(BF16) for c in chains}
        for _ in range(5):
            p = {c: _dot(pb[c], pb[c]) for c in chains}
            pb = {c: p[c].astype(BF16) for c in chains}
            e = {c: e[c] + p[c] + _dot(e[c].astype(BF16), pb[c]) for c in chains}
        eg = {c: jnp.exp(gc[c]) for c in chains}
        rhs = {c: jnp.concatenate([v[c].astype(F32) * beta[c], kb[c] * eg[c]], axis=1) for c in chains}
        uwb = {c: (rhs[c] + _dot(e[c].astype(BF16), rhs[c].astype(BF16))).astype(BF16) for c in chains}
        a_uw = {c: _dot(a_qk[c], uwb[c]) for c in chains}
        q_eff = {c: (q[c].astype(F32) * eg[c] - a_uw[c][:, HEAD_DIM:]).astype(BF16) for c in chains}
        k_dec = {}
        for c in chains:
            g_last = jnp.where(second, gc[c][PAIR - 1:PAIR, :], gc[c][CHUNK - 1:CHUNK, :])
            k_dec[c] = (kf[c] * jnp.exp(g_last - gc[c])).astype(BF16)
        kuw = {(c, h2): _dot_tn(k_dec[c][half(h2)], uwb[c][half(h2)]) for c in chains for h2 in halves}
        kwb = {key: val[:, HEAD_DIM:].astype(BF16) for key, val in kuw.items()}
        state = [s_ref[hh] for hh in heads]
        for d in pairs:
            outs = [[] for _ in heads]
            for h2 in halves:
                sb = [st.astype(BF16) for st in state]
                for hh in heads:
                    outs[hh].append(a_uw[d, hh][half(h2), :HEAD_DIM] + _dot(q_eff[d, hh][half(h2)], sb[hh]))
                state = [state[hh] * jnp.exp(gc[d, hh][(h2 + 1) * CHUNK - 1:(h2 + 1) * CHUNK, :])
                         + kuw[(d, hh), h2][:, :HEAD_DIM] - _dot(kwb[(d, hh), h2], sb[hh]) for hh in heads]
            for hh in heads:
                o = jnp.concatenate(outs[hh], axis=0)
                o_ref[0, rows[d], hh * HEAD_DIM:(hh + 1) * HEAD_DIM] = _rms_rows(o, on_ref[...]).astype(o_ref.dtype)
        for hh in heads:
            s_ref[hh] = state[hh]
        return 0

    assert (tm // PAIR) % GDN_GROUP == 0
    lax.fori_loop(0, tm // PAIR // GDN_GROUP, group_body, 0)


def _gdn_call(gqkv, gb, out_norm):
    bsz, t_len, _ = gqkv.shape
    tm = PROJ_ROWS
    row = lambda w: pl.BlockSpec((1, tm, w), lambda b, t: (b, t, 0))
    return pl.pallas_call(
        _gdn_kernel,
        grid=(bsz, t_len // tm),
        in_specs=[row(3 * BRANCH_WIDTH), row(LANES), pl.BlockSpec(out_norm.shape, lambda b, t: (0, 0))],
        out_specs=row(BRANCH_WIDTH),
        out_shape=jax.ShapeDtypeStruct((bsz, t_len, BRANCH_WIDTH), BF16),
        scratch_shapes=[pltpu.VMEM((N_HEADS, HEAD_DIM, HEAD_DIM), F32)],
        compiler_params=_compiler_params(("arbitrary", "arbitrary")),
        name="gated_deltanet",
    )(gqkv, gb, out_norm)


def _hgrn_kernel(hq_ref, hf_ref, hi_ref, lb_ref, on_ref, o_ref, s_ref):
    t = pl.program_id(1)
    tm = hq_ref.shape[1]

    @pl.when(t == 0)
    def _():
        s_ref[...] = jnp.zeros_like(s_ref)

    ii, jj = _chunk_consts()
    incl = (ii >= jj).astype(BF16)
    levels = []
    half = CHUNK // 2
    while half >= HG_SUB:
        row_upper = (ii % (2 * half)) >= half
        same_group = (ii // (2 * half)) == (jj // (2 * half))
        col_lower = (jj % (2 * half)) < half
        levels.append((half, row_upper & same_group & col_lower))
        half //= 2
    sub_lane = lax.broadcasted_iota(jnp.int32, (HG_SUB, CHUNK), 1)

    heads = range(N_HEADS)
    hs = lambda hh: slice(hh * HEAD_DIM, (hh + 1) * HEAD_DIM)

    def group_body(grp, _):
        chunks = range(HG_GROUP)
        chains = [(c, hh) for c in chunks for hh in heads]
        rows = [pl.ds(pl.multiple_of((grp * HG_GROUP + c) * CHUNK, CHUNK), CHUNK) for c in chunks]
        lbh = [lb_ref[:, hs(hh)] for hh in heads]
        sig = {(c, hh): _sigmoid(hf_ref[0, rows[c], hs(hh)]) for c, hh in chains}
        k = {(c, hh): (1.0 - lbh[hh]) * (1.0 - sig[c, hh]) for c, hh in chains}
        g_cum = {(c, hh): _dot_01(incl, jnp.log(lbh[hh] + (1.0 - lbh[hh]) * sig[c, hh])) for c, hh in chains}
        q = {(c, hh): hq_ref[0, rows[c], hs(hh)].astype(F32) for c, hh in chains}
        vb = {(c, hh): hi_ref[0, rows[c], hs(hh)] for c, hh in chains}
        g_end = {ch: g_cum[ch][CHUNK - 1:CHUNK, :] for ch in chains}
        k_dec = {ch: (k[ch] * jnp.exp(g_end[ch] - g_cum[ch])).astype(BF16) for ch in chains}
        gain_t = {ch: _dot_tn(vb[ch], k_dec[ch]) for ch in chains}
        q_dec = {ch: (q[ch] * jnp.exp(g_cum[ch])).astype(BF16) for ch in chains}

        a_off = {ch: jnp.zeros((CHUNK, CHUNK), F32) for ch in chains}
        for half, lvl_mask in levels:
            for ch in chains:
                ref_rows = jnp.concatenate(
                    [jnp.broadcast_to(g_cum[ch][g0 + half:g0 + half + 1, :], (2 * half, HEAD_DIM))
                     for g0 in range(0, CHUNK, 2 * half)], axis=0)
                q_s = q[ch] * jnp.exp(jnp.minimum(g_cum[ch] - ref_rows, 0.0))
                k_s = k[ch] * jnp.exp(jnp.minimum(ref_rows - g_cum[ch], 0.0))
                a_off[ch] = a_off[ch] + jnp.where(lvl_mask, _dot_nt(q_s.astype(BF16), k_s.astype(BF16)), 0.0)
        o_intra = {}
        for ch in chains:
            blocks = []
            for b0 in range(0, CHUNK, HG_SUB):
                g_blk = g_cum[ch][b0:b0 + HG_SUB, :]
                q_blk = q[ch][b0:b0 + HG_SUB, :]
                a_blk = jnp.zeros((HG_SUB, CHUNK), F32)
                for j in range(b0, b0 + HG_SUB):
                    dec = jnp.exp(jnp.minimum(g_blk - g_cum[ch][j:j + 1, :], 0.0))
                    col = jnp.sum(dec * (q_blk * k[ch][j:j + 1, :]), axis=-1, keepdims=True)
                    a_blk = jnp.where(sub_lane == j, col, a_blk)
                blocks.append(a_blk)
            a_diag = jnp.where(ii >= jj, jnp.concatenate(blocks, axis=0), 0.0)
            o_intra[ch] = _dot((a_off[ch] + a_diag).astype(BF16), vb[ch])

        state_t = [s_ref[hh] for hh in heads]
        for c in chunks:
            for hh in heads:
                o = o_intra[c, hh] + _dot_nt(q_dec[c, hh], state_t[hh].astype(BF16))
                o_ref[0, rows[c], hs(hh)] = _rms_rows(o, on_ref[...]).astype(o_ref.dtype)
            state_t = [state_t[hh] * jnp.exp(g_end[c, hh]) + gain_t[c, hh] for hh in heads]
        for hh in heads:
            s_ref[hh] = state_t[hh]
        return 0

    assert (tm // CHUNK) % HG_GROUP == 0
    lax.fori_loop(0, tm // CHUNK // HG_GROUP, group_body, 0)


def _hgrn_call(hq, hf, hi, lower_bound, out_norm):
    bsz, t_len, bw = hq.shape
    tm = PROJ_ROWS
    row = pl.BlockSpec((1, tm, bw), lambda b, t: (b, t, 0))
    full = lambda a: pl.BlockSpec(a.shape, lambda b, t: (0, 0))
    return pl.pallas_call(
        _hgrn_kernel,
        grid=(bsz, t_len // tm),
        in_specs=[row, row, row, full(lower_bound), full(out_norm)],
        out_specs=row,
        out_shape=jax.ShapeDtypeStruct((bsz, t_len, bw), BF16),
        scratch_shapes=[pltpu.VMEM((N_HEADS, HEAD_DIM, HEAD_DIM), F32)],
        compiler_params=_compiler_params(("arbitrary", "arbitrary")),
        name="hgrn2",
    )(hq, hf, hi, lower_bound, out_norm)


def _merge_kernel(h_ref, osb_ref, ogd_ref, ohg_ref, nw_ref, wz_ref, wmix_ref, wb_ref, wo_ref, out_ref):
    x = h_ref[...]
    xb = _rms_rows(x, nw_ref[...]).astype(BF16)
    y = None
    for b, o_ref in enumerate((osb_ref, ogd_ref, ohg_ref)):
        zg = _dot(xb, wz_ref[:, b * BRANCH_WIDTH:(b + 1) * BRANCH_WIDTH])
        gated = (o_ref[...].astype(F32) * _silu(zg)).astype(BF16)
        mix = _sigmoid(_dot(xb, wmix_ref[:, b * D_MODEL:(b + 1) * D_MODEL]))
        term = mix * _dot(gated, wb_ref[b])
        y = term if y is None else y + term
    out_ref[...] = x + _dot(y.astype(BF16), wo_ref[...])


def _merge_call(h2, osb, ogd, ohg, lw):
    rows, d = h2.shape
    tm = MERGE_ROWS
    assert rows % tm == 0
    row = lambda w: pl.BlockSpec((tm, w), lambda i: (i, 0))
    full = lambda a: pl.BlockSpec(a.shape, lambda i: (0,) * a.ndim)
    weights = (lw["norm_w"], lw["w_z"], lw["w_mix"], lw["w_branch"], lw["w_out"])
    return pl.pallas_call(
        _merge_kernel,
        grid=(rows // tm,),
        in_specs=[row(d), row(BRANCH_WIDTH), row(BRANCH_WIDTH), row(BRANCH_WIDTH)] + [full(w) for w in weights],
        out_specs=row(d),
        out_shape=jax.ShapeDtypeStruct((rows, d), F32),
        compiler_params=_compiler_params(("arbitrary",)),
        name="merge",
    )(h2, osb, ogd, ohg, *weights)


def _layer_weights(layer, norm_w, w_in, sb_q_norm, sb_k_norm, gdn_conv_w, gdn_a_log, gdn_dt_bias, w_branch, w_out):
    bw = BRANCH_WIDTH
    w = w_in[layer]
    o_gqkv = 4 * bw
    o_gz = o_gqkv + 3 * bw
    o_gb = o_gz + bw
    o_hq = o_gb + 2 * N_HEADS
    o_hz = o_hq + 3 * bw
    o_mix = o_hz + bw
    w_ba = jnp.zeros((D_MODEL, LANES), F32).at[:, :2 * N_HEADS].set(w[:, o_gb:o_gb + 2 * N_HEADS])
    pad = LANES - 2 * N_HEADS
    neg_a = jnp.concatenate([jnp.zeros((N_HEADS,), F32), -jnp.exp(gdn_a_log[layer].astype(F32)), jnp.zeros((pad,), F32)])
    dt_b = jnp.concatenate([jnp.zeros((N_HEADS,), F32), gdn_dt_bias[layer].astype(F32), jnp.zeros((pad,), F32)])
    return {
        "norm_w": norm_w[layer].astype(F32)[None, :],
        "w_qv_t": jnp.concatenate([w[:, 0:bw], w[:, 2 * bw:3 * bw]], axis=1).T.astype(BF16),
        "w_k": w[:, bw:2 * bw].astype(BF16),
        "w_gqkv": w[:, o_gqkv:o_gqkv + 3 * bw].astype(BF16),
        "w_ba": w_ba.astype(BF16),
        "w_hg": w[:, o_hq:o_hq + 3 * bw].astype(BF16),
        "sb_qn": (sb_q_norm[layer].astype(F32) * (HEAD_DIM ** -0.5 * LOG2E))[:, None],
        "sb_kn": sb_k_norm[layer].astype(F32)[None, :],
        "conv_w": gdn_conv_w[layer].astype(F32),
        "g_coef": jnp.stack([neg_a, dt_b]),
        "w_z": jnp.concatenate([w[:, 3 * bw:4 * bw], w[:, o_gz:o_gz + bw], w[:, o_hz:o_hz + bw]], axis=1).astype(BF16),
        "w_mix": w[:, o_mix:].astype(BF16),
        "w_branch": w_branch[layer].astype(BF16),
        "w_out": w_out[layer].astype(BF16),
    }


def kernel(x, meta_tokens, norm_w, w_in, sb_q_norm, sb_k_norm, gdn_conv_w, gdn_a_log, gdn_dt_bias, gdn_out_norm,
           hgrn_lb_logits, hgrn_out_norm, w_branch, w_out):
    bsz, _, d = x.shape
    depth = w_in.shape[0]
    h = jnp.concatenate([
        jnp.zeros((bsz, PAD_FRONT, d), x.dtype),
        jnp.broadcast_to(meta_tokens.astype(x.dtype)[None], (bsz, N_META, d)),
        x], axis=1)
    t_len = h.shape[1]
    p = jax.nn.softmax(hgrn_lb_logits.astype(F32), axis=0)
    lower_bounds = jnp.cumsum(p, axis=0) - p[0:1]
    for layer in range(depth):
        lw = _layer_weights(layer, norm_w, w_in, sb_q_norm, sb_k_norm, gdn_conv_w, gdn_a_log, gdn_dt_bias,
                            w_branch, w_out)
        qt, k, vt, gqkv, gb, hq, hf, hi = _proj_call(h, lw)
        o_sb = _sb_call(qt, k, vt)
        o_gd = _gdn_call(gqkv, gb, gdn_out_norm[layer].astype(F32)[None, :])
        o_hg = _hgrn_call(hq, hf, hi, lower_bounds[layer][None, :], hgrn_out_norm[layer].astype(F32)[None, :])
        flat = lambda a: a.reshape(bsz * t_len, a.shape[-1])
        h = _merge_call(flat(h), flat(o_sb), flat(o_gd), flat(o_hg), lw).reshape(bsz, t_len, d)
    return h[:, FRONT:]
```

```python
import functools

import jax
import jax.numpy as jnp
from jax import lax
from jax.experimental import pallas as pl
from jax.experimental.pallas import tpu as pltpu

F32 = jnp.float32
BF16 = jnp.bfloat16

D_MODEL = 1024
BRANCH_WIDTH = D_MODEL // 2
HEAD_DIM = 128
N_HEADS = BRANCH_WIDTH // HEAD_DIM
N_BRANCHES = 3
CONV_WIDTH = 4
CHUNK = 64
N_META = 16
FRONT = 128
PAD_FRONT = FRONT - N_META
NORM_EPS = 1e-6

LANES = 128
SUBLANES = 8
VMEM_LIMIT_BYTES = 56 * 1024 * 1024

PROJ_ROWS = 640
PROJ_PIECES = (256, 256, 128)
MERGE_ROWS = 512
SB_TILE = 256
SB_SUB = 128
SB_HEADS = 2
SB_DEAD = 160.0
PAIR = 2 * CHUNK
GDN_GROUPS = (5,)
HG_GROUP = 2
HG_SUB = 8
LOG2E = 1.4426950408889634


def _dot(a, b):
    return jnp.dot(a, b, preferred_element_type=F32)


def _dot_nt(a, b):
    return lax.dot_general(a, b, (((1,), (1,)), ((), ())), preferred_element_type=F32)


def _dot_tn(a, b):
    return lax.dot_general(a, b, (((0,), (0,)), ((), ())), preferred_element_type=F32)


def _sigmoid(x):
    return 1.0 / (1.0 + jnp.exp(-x))


def _silu(x):
    return x * _sigmoid(x)


def _rms_rows(x, w):
    return x * lax.rsqrt(jnp.mean(x * x, axis=-1, keepdims=True) + NORM_EPS) * w


def _compiler_params(semantics):
    return pltpu.CompilerParams(dimension_semantics=semantics, vmem_limit_bytes=VMEM_LIMIT_BYTES)


def _proj_kernel(h_ref, nw_ref, wqv_ref, wk_ref, wg_ref, wba_ref, whg_ref, qn_ref, kn_ref, conv_ref, gco_ref,
                 qt_ref, k_ref, vt_ref, gqkv_ref, gb_ref, hq_ref, hf_ref, hi_ref,
                 conv_buf, hg_buf, qv_buf, k_buf, ba_buf):
    t = pl.program_id(1)
    tm = h_ref.shape[1]
    bw = BRANCH_WIDTH
    first = SUBLANES - (CONV_WIDTH - 1)

    @pl.when(t == 0)
    def _():
        conv_buf[0:SUBLANES, :] = jnp.zeros((SUBLANES, 3 * bw), F32)

    assert sum(PROJ_PIECES) == tm
    starts = [sum(PROJ_PIECES[:r]) for r in range(len(PROJ_PIECES))]

    def project(r):
        r0, n = starts[r], PROJ_PIECES[r]
        rows = slice(r0, r0 + n)
        xb = _rms_rows(h_ref[0, rows, :], nw_ref[...]).astype(BF16)
        conv_buf[SUBLANES + r0:SUBLANES + r0 + n, :] = _dot(xb, wg_ref[...])
        hg_buf[rows, :] = _dot(xb, whg_ref[...])
        qv_buf[:, rows] = _dot_nt(wqv_ref[...], xb)
        k_buf[rows, :] = _dot(xb, wk_ref[...])
        ba_buf[rows, :] = _dot(xb, wba_ref[...])

    def pointwise(r):
        r0, n = starts[r], PROJ_PIECES[r]
        rows = slice(r0, r0 + n)
        pos = t * tm + r0 + lax.broadcasted_iota(jnp.int32, (n, 1), 0)
        vmask = (pos >= PAD_FRONT).astype(F32)

        base = first + r0
        y = conv_buf[base:base + n, :] * conv_ref[0:1, :]
        for i in range(1, CONV_WIDTH):
            y = y + conv_buf[base + i:base + i + n, :] * conv_ref[i:i + 1, :]
        y = _silu(y)
        for j in range(2 * N_HEADS):
            sl = slice(j * HEAD_DIM, (j + 1) * HEAD_DIM)
            yh = y[:, sl]
            yh = yh * lax.rsqrt(jnp.sum(yh * yh, axis=-1, keepdims=True) + NORM_EPS)
            if j < N_HEADS:
                yh = yh * (HEAD_DIM ** -0.5)
            gqkv_ref[0, rows, sl] = yh.astype(BF16)
        gqkv_ref[0, rows, 2 * bw:] = y[:, 2 * bw:].astype(BF16)

        hg = hg_buf[rows, :]
        hq_ref[0, rows, :] = _silu(hg[:, :bw]).astype(BF16)
        hf_ref[0, rows, :] = hg[:, bw:2 * bw]
        hi_ref[0, rows, :] = (hg[:, 2 * bw:] * vmask).astype(BF16)

        for hh in range(N_HEADS):
            sl = slice(hh * HEAD_DIM, (hh + 1) * HEAD_DIM)
            qh = qv_buf[sl, rows]
            ms = jnp.mean(qh * qh, axis=0, keepdims=True)
            qt_ref[0, sl, rows] = (qh * lax.rsqrt(ms + NORM_EPS) * qn_ref[...]).astype(BF16)
            k_ref[0, rows, sl] = _rms_rows(k_buf[rows, sl], kn_ref[...]).astype(BF16)
        vt_ref[0, :, rows] = qv_buf[bw:, rows].astype(BF16)

        ba = ba_buf[rows, :]
        sp_in = ba + gco_ref[1:2, :]
        softplus = jnp.maximum(sp_in, 0.0) + jnp.log(1.0 + jnp.exp(-jnp.abs(sp_in)))
        lane = lax.broadcasted_iota(jnp.int32, ba.shape, 1)
        gb_ref[0, rows, :] = jnp.where(lane < N_HEADS, _sigmoid(ba) * vmask, gco_ref[0:1, :] * softplus)

    n_sub = len(PROJ_PIECES)
    project(0)
    for r in range(n_sub):
        if r + 1 < n_sub:
            project(r + 1)
        pointwise(r)
    conv_buf[0:SUBLANES, :] = conv_buf[tm:tm + SUBLANES, :]


def _proj_call(h, lw):
    bsz, t_len, d = h.shape
    tm = PROJ_ROWS
    assert t_len % tm == 0
    full = lambda a: pl.BlockSpec(a.shape, lambda b, t: (0,) * a.ndim)
    row = lambda w: pl.BlockSpec((1, tm, w), lambda b, t: (b, t, 0))
    col = lambda w: pl.BlockSpec((1, w, tm), lambda b, t: (b, 0, t))
    weights = (lw["norm_w"], lw["w_qv_t"], lw["w_k"], lw["w_gqkv"], lw["w_ba"], lw["w_hg"], lw["sb_qn"], lw["sb_kn"],
               lw["conv_w"], lw["g_coef"])
    bw = BRANCH_WIDTH
    out_shape = (
        jax.ShapeDtypeStruct((bsz, bw, t_len), BF16),
        jax.ShapeDtypeStruct((bsz, t_len, bw), BF16),
        jax.ShapeDtypeStruct((bsz, bw, t_len), BF16),
        jax.ShapeDtypeStruct((bsz, t_len, 3 * bw), BF16),
        jax.ShapeDtypeStruct((bsz, t_len, LANES), F32),
        jax.ShapeDtypeStruct((bsz, t_len, bw), BF16),
        jax.ShapeDtypeStruct((bsz, t_len, bw), F32),
        jax.ShapeDtypeStruct((bsz, t_len, bw), BF16),
    )
    out_specs = (col(bw), row(bw), col(bw), row(3 * bw), row(LANES), row(bw), row(bw), row(bw))
    return pl.pallas_call(
        _proj_kernel,
        grid=(bsz, t_len // tm),
        in_specs=[row(d)] + [full(w) for w in weights],
        out_specs=out_specs,
        out_shape=out_shape,
        scratch_shapes=[pltpu.VMEM((tm + SUBLANES, 3 * bw), F32), pltpu.VMEM((tm, 3 * bw), F32),
                        pltpu.VMEM((2 * bw, tm), F32), pltpu.VMEM((tm, bw), F32), pltpu.VMEM((tm, LANES), F32)],
        compiler_params=_compiler_params(("arbitrary", "arbitrary")),
        name="proj",
    )(h, *weights)


def _sb_kernel(qt_ref, k_ref, vt_ref, o_ref, acc_ref, carry_ref, z_buf, attn_buf):
    t_len = k_ref.shape[1]
    n_tiles = (t_len - FRONT) // SB_TILE
    ii = lax.broadcasted_iota(jnp.int32, (SB_SUB, SB_SUB), 0)
    jj = lax.broadcasted_iota(jnp.int32, (SB_SUB, SB_SUB), 1)
    later = (jj > ii).astype(BF16)
    heads = range(SB_HEADS)
    hs = lambda hd: slice(hd * HEAD_DIM, (hd + 1) * HEAD_DIM)
    pair = [SB_TILE, SB_TILE]

    def visit_order(widths):
        return [(bi, hd) for bi in reversed(range(len(widths))) for hd in heads]

    def scores(q0, nq, k_lo, widths):
        starts = [sum(widths[:i]) for i in range(len(widths))]
        return {(bi, hd): _dot(k_ref[0, pl.ds(k_lo + starts[bi], widths[bi]), hs(hd)], qt_ref[0, hs(hd), pl.ds(q0, nq)])
                for bi, hd in visit_order(widths)}

    def weights(z, q0, nq, k_lo, widths, causal, front):
        starts = [sum(widths[:i]) for i in range(len(widths))]
        order = visit_order(widths)
        mask = {}
        for bi in range(len(widths)):
            mask[bi] = None
            inert = front and k_lo + starts[bi] < PAD_FRONT
            if bi in causal or inert:
                s_idx = k_lo + starts[bi] + lax.broadcasted_iota(jnp.int32, (widths[bi], nq), 0)
                if bi in causal:
                    mask[bi] = s_idx < q0 + lax.broadcasted_iota(jnp.int32, (widths[bi], nq), 1)
                if inert:
                    valid = s_idx >= PAD_FRONT
                    mask[bi] = valid if mask[bi] is None else mask[bi] & valid
        log_beta, keep = {}, {}
        for key in order:
            zz = z[key]
            nz = -zz
            log_keep = jnp.minimum(nz, 0.0) - jnp.log(1.0 + jnp.exp2(jnp.minimum(zz, nz))) * LOG2E
            log_beta[key] = zz + log_keep
            if mask[key[0]] is not None:
                log_keep = jnp.where(mask[key[0]], log_keep, 0.0)
            keep[key] = log_keep.astype(BF16)
        passed = {}
        for bi, hd in order:
            for s in reversed(range(0, widths[bi], SB_SUB)):
                passed[bi, hd, s] = _dot(later, keep[bi, hd][s:s + SB_SUB])
        out = []
        for hd in heads:
            carry = carry_ref[hd, :, :nq]
            parts = []
            for bi in reversed(range(len(widths))):
                for s in reversed(range(0, widths[bi], SB_SUB)):
                    p = passed[bi, hd, s]
                    attn = jnp.exp2(log_beta[bi, hd][s:s + SB_SUB] + p + carry)
                    if mask[bi] is not None:
                        attn = jnp.where(mask[bi][s:s + SB_SUB], attn, 0.0)
                    parts.insert(0, attn.astype(BF16))
                    carry = carry + p[0:1, :] + keep[bi, hd][s:s + 1, :].astype(F32)
            carry_ref[hd, :, :nq] = carry
            out.append(parts[0] if len(parts) == 1 else jnp.concatenate(parts, axis=0))
        return out

    def accumulate(attn, nq, k_lo, total):
        for hd in heads:
            acc_ref[hd, :, :nq] += _dot(vt_ref[0, hs(hd), pl.ds(k_lo, total)], attn[hd])

    def visit(q0, nq, k_lo, widths, causal, front):
        z = scores(q0, nq, k_lo, widths)
        accumulate(weights(z, q0, nq, k_lo, widths, causal, front), nq, k_lo, sum(widths))

    def start():
        acc_ref[...] = jnp.zeros_like(acc_ref)
        carry_ref[...] = jnp.zeros_like(carry_ref)

    def finish(q0, nq):
        for hd in heads:
            o_ref[0, pl.ds(q0, nq), hs(hd)] = acc_ref[hd, :, :nq].T.astype(o_ref.dtype)

    def alive():
        return (jnp.max(carry_ref[...]) > -SB_DEAD).astype(jnp.int32)

    start()
    visit(0, FRONT, 0, [FRONT], [0], True)
    finish(0, FRONT)

    def q_body(m, _):
        q0 = pl.multiple_of(FRONT + m * SB_TILE, LANES)
        start()

        @pl.when(m == 0)
        def _():
            visit(q0, SB_TILE, q0, [SB_TILE], [0], False)

        @pl.when(m > 0)
        def _():
            visit(q0, SB_TILE, q0 - SB_TILE, pair, [1], False)

        rest = jnp.maximum(m - 1, 0)
        n_quads = lax.shift_right_logical(rest, 2)
        last_pair = 2 * n_quads - 1
        pair_lo = lambda i: pl.multiple_of(q0 - (2 * i + 3) * SB_TILE, LANES)

        def put_scores(slot, k_lo):
            for key, zz in scores(q0, SB_TILE, k_lo, pair).items():
                z_buf[slot, key[0], key[1]] = zz

        def accumulate_from(slot, k_lo):
            accumulate([attn_buf[slot, hd] for hd in heads], SB_TILE, k_lo, 2 * SB_TILE)

        def pair_step(slot, i):
            accumulate_from(1 - slot, pair_lo(jnp.maximum(i - 1, 0)))
            put_scores(1 - slot, pair_lo(jnp.minimum(i + 1, last_pair)))
            z = {key: z_buf[slot, key[0], key[1]] for key in visit_order(pair)}
            attn = weights(z, q0, SB_TILE, pair_lo(i), pair, [], False)
            for hd in heads:
                attn_buf[slot, hd] = attn[hd]

        @pl.when(alive() == 1)
        def _():
            @pl.when(n_quads > 0)
            def _():
                attn_buf[1] = jnp.zeros_like(attn_buf[1])
                put_scores(0, pair_lo(0))

                def quad_step(state):
                    j, _ = state
                    pair_step(0, 2 * j)
                    pair_step(1, 2 * j + 1)
                    return j + 1, alive()

                done, _ = lax.while_loop(lambda s: (s[0] < n_quads) & (s[1] == 1), quad_step,
                                         (jnp.int32(0), jnp.int32(1)))
                accumulate_from(1, pair_lo(2 * done - 1))

            still_alive = alive()
            for left in range(min(4, n_tiles)):
                @pl.when(((rest & 3) == left) & (still_alive == 1))
                def _():
                    visit(q0, SB_TILE, 0, [FRONT] + [SB_TILE] * left, [], True)

        finish(q0, SB_TILE)
        return 0

    lax.fori_loop(0, n_tiles, q_body, 0)


def _sb_call(qt, k, vt):
    bsz, t_len, bw = k.shape
    assert (t_len - FRONT) % SB_TILE == 0 and N_HEADS % SB_HEADS == 0
    hw = SB_HEADS * HEAD_DIM
    col = pl.BlockSpec((1, hw, t_len), lambda b, h: (b, h, 0))
    row = pl.BlockSpec((1, t_len, hw), lambda b, h: (b, 0, h))
    return pl.pallas_call(
        _sb_kernel,
        grid=(bsz, N_HEADS // SB_HEADS),
        in_specs=[col, row, col],
        out_specs=row,
        out_shape=jax.ShapeDtypeStruct((bsz, t_len, bw), BF16),
        scratch_shapes=[pltpu.VMEM((SB_HEADS, HEAD_DIM, SB_TILE), F32), pltpu.VMEM((SB_HEADS, 1, SB_TILE), F32),
                        pltpu.VMEM((2, 2, SB_HEADS, SB_TILE, SB_TILE), F32),
                        pltpu.VMEM((2, SB_HEADS, 2 * SB_TILE, SB_TILE), BF16)],
        compiler_params=_compiler_params(("arbitrary", "arbitrary")),
        name="stick_breaking",
    )(qt, k, vt)


def _chunk_consts():
    ii = lax.broadcasted_iota(jnp.int32, (CHUNK, CHUNK), 0)
    jj = lax.broadcasted_iota(jnp.int32, (CHUNK, CHUNK), 1)
    return ii, jj


def _split3(x):
    hi = x.astype(BF16)
    r = x - hi.astype(F32)
    mid = r.astype(BF16)
    return hi, mid, (r - mid.astype(F32)).astype(BF16)


def _dot_01(tri, x):
    hi, mid, lo = _split3(x)
    return _dot(tri, hi) + _dot(tri, mid) + _dot(tri, lo)


def _pair_masks():
    ii = lax.broadcasted_iota(jnp.int32, (PAIR, PAIR), 0)
    jj = lax.broadcasted_iota(jnp.int32, (PAIR, PAIR), 1)
    same = (ii // CHUNK) == (jj // CHUNK)
    return same & (ii >= jj), same & (ii > jj)


def _gdn_kernel(gqkv_ref, gb_ref, on_ref, o_ref, s_ref):
    t = pl.program_id(1)
    tm = gqkv_ref.shape[1]
    bw = BRANCH_WIDTH

    @pl.when(t == 0)
    def _():
        s_ref[...] = jnp.zeros_like(s_ref)

    causal, strict = _pair_masks()
    incl = causal.astype(BF16)
    second = lax.broadcasted_iota(jnp.int32, (PAIR, 1), 0) >= CHUNK

    heads = range(N_HEADS)
    halves = range(PAIR // CHUNK)
    half = lambda c: slice(c * CHUNK, (c + 1) * CHUNK)

    def group(first, count, state):
        pairs = range(count)
        chains = [(d, hh) for d in pairs for hh in heads]
        rows = [slice((first + d) * PAIR, (first + d + 1) * PAIR) for d in pairs]
        gbc = [gb_ref[0, rows[d], :] for d in pairs]
        g_cum = [_dot_01(incl, gbc[d]) for d in pairs]
        g_cum_t = [g.T for g in g_cum]
        q = {(d, hh): gqkv_ref[0, rows[d], hh * HEAD_DIM:(hh + 1) * HEAD_DIM] for d, hh in chains}
        k = {(d, hh): gqkv_ref[0, rows[d], bw + hh * HEAD_DIM:bw + (hh + 1) * HEAD_DIM] for d, hh in chains}
        v = {(d, hh): gqkv_ref[0, rows[d], 2 * bw + hh * HEAD_DIM:2 * bw + (hh + 1) * HEAD_DIM] for d, hh in chains}
        kf = {c: k[c].astype(F32) for c in chains}
        beta = {(d, hh): gbc[d][:, hh:hh + 1] for d, hh in chains}
        gc = {(d, hh): g_cum[d][:, N_HEADS + hh:N_HEADS + hh + 1] for d, hh in chains}
        decay = {(d, hh): jnp.exp(jnp.minimum(gc[d, hh] - g_cum_t[d][N_HEADS + hh:N_HEADS + hh + 1, :], 0.0))
                 for d, hh in chains}
        kb = {c: kf[c] * beta[c] for c in chains}
        m = {c: jnp.where(strict, _dot_nt(kb[c].astype(BF16), k[c]) * decay[c], 0.0) for c in chains}
        a_qk = {c: jnp.where(causal, _dot_nt(q[c], k[c]) * decay[c], 0.0).astype(BF16) for c in chains}
        e = {c: -m[c] for c in chains}
        pb = {c: m[c].astype(BF16) for c in chains}
        for _ in range(5):
            p = {c: _dot(pb[c], pb[c]) for c in chains}
            pb = {c: p[c].astype(BF16) for c in chains}
            e = {c: e[c] + p[c] + _dot(e[c].astype(BF16), pb[c]) for c in chains}
        eg = {c: jnp.exp(gc[c]) for c in chains}
        rhs = {c: jnp.concatenate([v[c].astype(F32) * beta[c], kb[c] * eg[c]], axis=1) for c in chains}
        uwb = {c: (rhs[c] + _dot(e[c].astype(BF16), rhs[c].astype(BF16))).astype(BF16) for c in chains}
        a_uw = {c: _dot(a_qk[c], uwb[c]) for c in chains}
        q_eff = {c: (q[c].astype(F32) * eg[c] - a_uw[c][:, HEAD_DIM:]).astype(BF16) for c in chains}
        k_dec = {}
        for c in chains:
            g_last = jnp.where(second, gc[c][PAIR - 1:PAIR, :], gc[c][CHUNK - 1:CHUNK, :])
            k_dec[c] = (kf[c] * jnp.exp(g_last - gc[c])).astype(BF16)
        kuw = {(c, h2): _dot_tn(k_dec[c][half(h2)], uwb[c][half(h2)]) for c in chains for h2 in halves}
        kwb = {key: val[:, HEAD_DIM:].astype(BF16) for key, val in kuw.items()}
        for d in pairs:
            outs = [[] for _ in heads]
            for h2 in halves:
                sb = [st.astype(BF16) for st in state]
                for hh in heads:
                    outs[hh].append(a_uw[d, hh][half(h2), :HEAD_DIM] + _dot(q_eff[d, hh][half(h2)], sb[hh]))
                state = [state[hh] * jnp.exp(gc[d, hh][(h2 + 1) * CHUNK - 1:(h2 + 1) * CHUNK, :])
                         + kuw[(d, hh), h2][:, :HEAD_DIM] - _dot(kwb[(d, hh), h2], sb[hh]) for hh in heads]
            for hh in heads:
                o = jnp.concatenate(outs[hh], axis=0)
                o_ref[0, rows[d], hh * HEAD_DIM:(hh + 1) * HEAD_DIM] = _rms_rows(o, on_ref[...]).astype(o_ref.dtype)
        return state

    assert sum(GDN_GROUPS) * PAIR == tm
    state = [s_ref[hh] for hh in heads]
    first = 0
    for count in GDN_GROUPS:
        state = group(first, count, state)
        first += count
    for hh in heads:
        s_ref[hh] = state[hh]


def _gdn_call(gqkv, gb, out_norm):
    bsz, t_len, _ = gqkv.shape
    tm = PROJ_ROWS
    row = lambda w: pl.BlockSpec((1, tm, w), lambda b, t: (b, t, 0))
    return pl.pallas_call(
        _gdn_kernel,
        grid=(bsz, t_len // tm),
        in_specs=[row(3 * BRANCH_WIDTH), row(LANES), pl.BlockSpec(out_norm.shape, lambda b, t: (0, 0))],
        out_specs=row(BRANCH_WIDTH),
        out_shape=jax.ShapeDtypeStruct((bsz, t_len, BRANCH_WIDTH), BF16),
        scratch_shapes=[pltpu.VMEM((N_HEADS, HEAD_DIM, HEAD_DIM), F32)],
        compiler_params=_compiler_params(("arbitrary", "arbitrary")),
        name="gated_deltanet",
    )(gqkv, gb, out_norm)


def _hgrn_kernel(hq_ref, hf_ref, hi_ref, lb_ref, on_ref, o_ref, s_ref):
    t = pl.program_id(1)
    tm = hq_ref.shape[1]

    @pl.when(t == 0)
    def _():
        s_ref[...] = jnp.zeros_like(s_ref)

    ii, jj = _chunk_consts()
    incl = (ii >= jj).astype(BF16)
    levels = []
    half = CHUNK // 2
    while half >= HG_SUB:
        row_upper = (ii % (2 * half)) >= half
        same_group = (ii // (2 * half)) == (jj // (2 * half))
        col_lower = (jj % (2 * half)) < half
        levels.append((half, row_upper & same_group & col_lower))
        half //= 2
    sub_lane = lax.broadcasted_iota(jnp.int32, (HG_SUB, CHUNK), 1)

    heads = range(N_HEADS)
    hs = lambda hh: slice(hh * HEAD_DIM, (hh + 1) * HEAD_DIM)

    def group_body(grp, _):
        chunks = range(HG_GROUP)
        chains = [(c, hh) for c in chunks for hh in heads]
        rows = [pl.ds(pl.multiple_of((grp * HG_GROUP + c) * CHUNK, CHUNK), CHUNK) for c in chunks]
        lbh = [lb_ref[:, hs(hh)] for hh in heads]
        sig = {(c, hh): _sigmoid(hf_ref[0, rows[c], hs(hh)]) for c, hh in chains}
        k = {(c, hh): (1.0 - lbh[hh]) * (1.0 - sig[c, hh]) for c, hh in chains}
        g_cum = {(c, hh): _dot_01(incl, jnp.log(lbh[hh] + (1.0 - lbh[hh]) * sig[c, hh])) for c, hh in chains}
        q = {(c, hh): hq_ref[0, rows[c], hs(hh)].astype(F32) for c, hh in chains}
        vb = {(c, hh): hi_ref[0, rows[c], hs(hh)] for c, hh in chains}
        g_end = {ch: g_cum[ch][CHUNK - 1:CHUNK, :] for ch in chains}
        k_dec = {ch: (k[ch] * jnp.exp(g_end[ch] - g_cum[ch])).astype(BF16) for ch in chains}
        gain_t = {ch: _dot_tn(vb[ch], k_dec[ch]) for ch in chains}
        q_dec = {ch: (q[ch] * jnp.exp(g_cum[ch])).astype(BF16) for ch in chains}

        a_off = {ch: jnp.zeros((CHUNK, CHUNK), F32) for ch in chains}
        for half, lvl_mask in levels:
            for ch in chains:
                ref_rows = jnp.concatenate(
                    [jnp.broadcast_to(g_cum[ch][g0 + half:g0 + half + 1, :], (2 * half, HEAD_DIM))
                     for g0 in range(0, CHUNK, 2 * half)], axis=0)
                q_s = q[ch] * jnp.exp(jnp.minimum(g_cum[ch] - ref_rows, 0.0))
                k_s = k[ch] * jnp.exp(jnp.minimum(ref_rows - g_cum[ch], 0.0))
                a_off[ch] = a_off[ch] + jnp.where(lvl_mask, _dot_nt(q_s.astype(BF16), k_s.astype(BF16)), 0.0)
        o_intra = {}
        for ch in chains:
            blocks = []
            for b0 in range(0, CHUNK, HG_SUB):
                g_blk = g_cum[ch][b0:b0 + HG_SUB, :]
                q_blk = q[ch][b0:b0 + HG_SUB, :]
                a_blk = jnp.zeros((HG_SUB, CHUNK), F32)
                for j in range(b0, b0 + HG_SUB):
                    dec = jnp.exp(jnp.minimum(g_blk - g_cum[ch][j:j + 1, :], 0.0))
                    col = jnp.sum(dec * (q_blk * k[ch][j:j + 1, :]), axis=-1, keepdims=True)
                    a_blk = jnp.where(sub_lane == j, col, a_blk)
                blocks.append(a_blk)
            a_diag = jnp.where(ii >= jj, jnp.concatenate(blocks, axis=0), 0.0)
            o_intra[ch] = _dot((a_off[ch] + a_diag).astype(BF16), vb[ch])

        state_t = [s_ref[hh] for hh in heads]
        for c in chunks:
            for hh in heads:
                o = o_intra[c, hh] + _dot_nt(q_dec[c, hh], state_t[hh].astype(BF16))
                o_ref[0, rows[c], hs(hh)] = _rms_rows(o, on_ref[...]).astype(o_ref.dtype)
            state_t = [state_t[hh] * jnp.exp(g_end[c, hh]) + gain_t[c, hh] for hh in heads]
        for hh in heads:
            s_ref[hh] = state_t[hh]
        return 0

    assert (tm // CHUNK) % HG_GROUP == 0
    lax.fori_loop(0, tm // CHUNK // HG_GROUP, group_body, 0)


def _hgrn_call(hq, hf, hi, lower_bound, out_norm):
    bsz, t_len, bw = hq.shape
    tm = PROJ_ROWS
    row = pl.BlockSpec((1, tm, bw), lambda b, t: (b, t, 0))
    full = lambda a: pl.BlockSpec(a.shape, lambda b, t: (0, 0))
    return pl.pallas_call(
        _hgrn_kernel,
        grid=(bsz, t_len // tm),
        in_specs=[row, row, row, full(lower_bound), full(out_norm)],
        out_specs=row,
        out_shape=jax.ShapeDtypeStruct((bsz, t_len, bw), BF16),
        scratch_shapes=[pltpu.VMEM((N_HEADS, HEAD_DIM, HEAD_DIM), F32)],
        compiler_params=_compiler_params(("arbitrary", "arbitrary")),
        name="hgrn2",
    )(hq, hf, hi, lower_bound, out_norm)


def _merge_kernel(h_ref, osb_ref, ogd_ref, ohg_ref, nw_ref, wz_ref, wmix_ref, wb_ref, wo_ref, out_ref):
    x = h_ref[...]
    xb = _rms_rows(x, nw_ref[...]).astype(BF16)
    y = None
    for b, o_ref in enumerate((osb_ref, ogd_ref, ohg_ref)):
        zg = _dot(xb, wz_ref[:, b * BRANCH_WIDTH:(b + 1) * BRANCH_WIDTH])
        gated = (o_ref[...].astype(F32) * _silu(zg)).astype(BF16)
        mix = _sigmoid(_dot(xb, wmix_ref[:, b * D_MODEL:(b + 1) * D_MODEL]))
        term = mix * _dot(gated, wb_ref[b])
        y = term if y is None else y + term
    out_ref[...] = x + _dot(y.astype(BF16), wo_ref[...])


def _merge_call(h2, osb, ogd, ohg, lw):
    rows, d = h2.shape
    tm = MERGE_ROWS
    assert rows % tm == 0
    row = lambda w: pl.BlockSpec((tm, w), lambda i: (i, 0))
    full = lambda a: pl.BlockSpec(a.shape, lambda i: (0,) * a.ndim)
    weights = (lw["norm_w"], lw["w_z"], lw["w_mix"], lw["w_branch"], lw["w_out"])
    return pl.pallas_call(
        _merge_kernel,
        grid=(rows // tm,),
        in_specs=[row(d), row(BRANCH_WIDTH), row(BRANCH_WIDTH), row(BRANCH_WIDTH)] + [full(w) for w in weights],
        out_specs=row(d),
        out_shape=jax.ShapeDtypeStruct((rows, d), F32),
        compiler_params=_compiler_params(("arbitrary",)),
        name="merge",
    )(h2, osb, ogd, ohg, *weights)


def _layer_weights(layer, norm_w, w_in, sb_q_norm, sb_k_norm, gdn_conv_w, gdn_a_log, gdn_dt_bias, w_branch, w_out):
    bw = BRANCH_WIDTH
    w = w_in[layer]
    o_gqkv = 4 * bw
    o_gz = o_gqkv + 3 * bw
    o_gb = o_gz + bw
    o_hq = o_gb + 2 * N_HEADS
    o_hz = o_hq + 3 * bw
    o_mix = o_hz + bw
    w_ba = jnp.zeros((D_MODEL, LANES), F32).at[:, :2 * N_HEADS].set(w[:, o_gb:o_gb + 2 * N_HEADS])
    pad = LANES - 2 * N_HEADS
    neg_a = jnp.concatenate([jnp.zeros((N_HEADS,), F32), -jnp.exp(gdn_a_log[layer].astype(F32)), jnp.zeros((pad,), F32)])
    dt_b = jnp.concatenate([jnp.zeros((N_HEADS,), F32), gdn_dt_bias[layer].astype(F32), jnp.zeros((pad,), F32)])
    return {
        "norm_w": norm_w[layer].astype(F32)[None, :],
        "w_qv_t": jnp.concatenate([w[:, 0:bw], w[:, 2 * bw:3 * bw]], axis=1).T.astype(BF16),
        "w_k": w[:, bw:2 * bw].astype(BF16),
        "w_gqkv": w[:, o_gqkv:o_gqkv + 3 * bw].astype(BF16),
        "w_ba": w_ba.astype(BF16),
        "w_hg": w[:, o_hq:o_hq + 3 * bw].astype(BF16),
        "sb_qn": (sb_q_norm[layer].astype(F32) * (HEAD_DIM ** -0.5 * LOG2E))[:, None],
        "sb_kn": sb_k_norm[layer].astype(F32)[None, :],
        "conv_w": gdn_conv_w[layer].astype(F32),
        "g_coef": jnp.stack([neg_a, dt_b]),
        "w_z": jnp.concatenate([w[:, 3 * bw:4 * bw], w[:, o_gz:o_gz + bw], w[:, o_hz:o_hz + bw]], axis=1).astype(BF16),
        "w_mix": w[:, o_mix:].astype(BF16),
        "w_branch": w_branch[layer].astype(BF16),
        "w_out": w_out[layer].astype(BF16),
    }


def kernel(x, meta_tokens, norm_w, w_in, sb_q_norm, sb_k_norm, gdn_conv_w, gdn_a_log, gdn_dt_bias, gdn_out_norm,
           hgrn_lb_logits, hgrn_out_norm, w_branch, w_out):
    bsz, _, d = x.shape
    depth = w_in.shape[0]
    h = jnp.concatenate([
        jnp.zeros((bsz, PAD_FRONT, d), x.dtype),
        jnp.broadcast_to(meta_tokens.astype(x.dtype)[None], (bsz, N_META, d)),
        x], axis=1)
    t_len = h.shape[1]
    p = jax.nn.softmax(hgrn_lb_logits.astype(F32), axis=0)
    lower_bounds = jnp.cumsum(p, axis=0) - p[0:1]
    for layer in range(depth):
        lw = _layer_weights(layer, norm_w, w_in, sb_q_norm, sb_k_norm, gdn_conv_w, gdn_a_log, gdn_dt_bias,
                            w_branch, w_out)
        qt, k, vt, gqkv, gb, hq, hf, hi = _proj_call(h, lw)
        o_sb = _sb_call(qt, k, vt)
        o_gd = _gdn_call(gqkv, gb, gdn_out_norm[layer].astype(F32)[None, :])
        o_hg = _hgrn_call(hq, hf, hi, lower_bounds[layer][None, :], hgrn_out_norm[layer].astype(F32)[None, :])
        flat = lambda a: a.reshape(bsz * t_len, a.shape[-1])
        h = _merge_call(flat(h), flat(o_sb), flat(o_gd), flat(o_hg), lw).reshape(bsz, t_len, d)
    return h[:, FRONT:]
```

```python
import collections

import jax
import jax.numpy as jnp
from jax import lax
from jax.experimental import pallas as pl
from jax.experimental.pallas import tpu as pltpu

F32 = jnp.float32
BF16 = jnp.bfloat16

D_MODEL = 1024
BRANCH_WIDTH = D_MODEL // 2
HEAD_DIM = 128
N_HEADS = BRANCH_WIDTH // HEAD_DIM
N_BRANCHES = 3
CONV_WIDTH = 4
CHUNK = 64
N_META = 16
FRONT = 128
PAD_FRONT = FRONT - N_META
NORM_EPS = 1e-6

LANES = 128
SUBLANES = 8
VMEM_LIMIT_BYTES = 56 * 1024 * 1024

PROJ_ROWS = 640
PROJ_PIECES = (256, 256, 128)
MERGE_ROWS = 512
SB_TILE = 256
SB_SUB = 128
SB_HEADS = 2
SB_DEAD = 160.0
PAIR = 2 * CHUNK
GDN_GROUPS = (5,)
HG_GROUP = 5
HG_SUB = 8
LOG2E = 1.4426950408889634


def _dot(a, b):
    return jnp.dot(a, b, preferred_element_type=F32)


def _dot_nt(a, b):
    return lax.dot_general(a, b, (((1,), (1,)), ((), ())), preferred_element_type=F32)


def _dot_tn(a, b):
    return lax.dot_general(a, b, (((0,), (0,)), ((), ())), preferred_element_type=F32)


def _sigmoid(x):
    return 1.0 / (1.0 + jnp.exp(-x))


def _silu(x):
    return x * _sigmoid(x)


def _rms_rows(x, w):
    return x * lax.rsqrt(jnp.mean(x * x, axis=-1, keepdims=True) + NORM_EPS) * w


def _compiler_params(semantics):
    return pltpu.CompilerParams(dimension_semantics=semantics, vmem_limit_bytes=VMEM_LIMIT_BYTES)


def _proj_kernel(h_ref, nw_ref, wqv_ref, wk_ref, wg_ref, wba_ref, whg_ref, qn_ref, kn_ref, conv_ref, gco_ref,
                 qt_ref, k_ref, vt_ref, gqkv_ref, gb_ref, hq_ref, hf_ref, hi_ref,
                 conv_buf, hg_buf, qv_buf, k_buf, ba_buf):
    t = pl.program_id(1)
    tm = h_ref.shape[1]
    bw = BRANCH_WIDTH
    first = SUBLANES - (CONV_WIDTH - 1)

    @pl.when(t == 0)
    def _():
        conv_buf[0:SUBLANES, :] = jnp.zeros((SUBLANES, 3 * bw), F32)

    assert sum(PROJ_PIECES) == tm
    starts = [sum(PROJ_PIECES[:r]) for r in range(len(PROJ_PIECES))]

    def project(r):
        r0, n = starts[r], PROJ_PIECES[r]
        rows = slice(r0, r0 + n)
        xb = _rms_rows(h_ref[0, rows, :], nw_ref[...]).astype(BF16)
        conv_buf[SUBLANES + r0:SUBLANES + r0 + n, :] = _dot(xb, wg_ref[...])
        hg_buf[rows, :] = _dot(xb, whg_ref[...])
        qv_buf[:, rows] = _dot_nt(wqv_ref[...], xb)
        k_buf[rows, :] = _dot(xb, wk_ref[...])
        ba_buf[rows, :] = _dot(xb, wba_ref[...])

    def pointwise(r):
        r0, n = starts[r], PROJ_PIECES[r]
        rows = slice(r0, r0 + n)
        pos = t * tm + r0 + lax.broadcasted_iota(jnp.int32, (n, 1), 0)
        vmask = (pos >= PAD_FRONT).astype(F32)

        base = first + r0
        y = conv_buf[base:base + n, :] * conv_ref[0:1, :]
        for i in range(1, CONV_WIDTH):
            y = y + conv_buf[base + i:base + i + n, :] * conv_ref[i:i + 1, :]
        y = _silu(y)
        for j in range(2 * N_HEADS):
            sl = slice(j * HEAD_DIM, (j + 1) * HEAD_DIM)
            yh = y[:, sl]
            yh = yh * lax.rsqrt(jnp.sum(yh * yh, axis=-1, keepdims=True) + NORM_EPS)
            if j < N_HEADS:
                yh = yh * (HEAD_DIM ** -0.5)
            gqkv_ref[0, rows, sl] = yh.astype(BF16)
        gqkv_ref[0, rows, 2 * bw:] = y[:, 2 * bw:].astype(BF16)

        hg = hg_buf[rows, :]
        hq_ref[0, rows, :] = _silu(hg[:, :bw]).astype(BF16)
        hf_ref[0, rows, :] = hg[:, bw:2 * bw]
        hi_ref[0, rows, :] = (hg[:, 2 * bw:] * vmask).astype(BF16)

        for hh in range(N_HEADS):
            sl = slice(hh * HEAD_DIM, (hh + 1) * HEAD_DIM)
            qh = qv_buf[sl, rows]
            ms = jnp.mean(qh * qh, axis=0, keepdims=True)
            qt_ref[0, sl, rows] = (qh * lax.rsqrt(ms + NORM_EPS) * qn_ref[...]).astype(BF16)
            k_ref[0, rows, sl] = _rms_rows(k_buf[rows, sl], kn_ref[...]).astype(BF16)
        vt_ref[0, :, rows] = qv_buf[bw:, rows].astype(BF16)

        ba = ba_buf[rows, :]
        sp_in = ba + gco_ref[1:2, :]
        softplus = jnp.maximum(sp_in, 0.0) + jnp.log(1.0 + jnp.exp(-jnp.abs(sp_in)))
        lane = lax.broadcasted_iota(jnp.int32, ba.shape, 1)
        gb_ref[0, rows, :] = jnp.where(lane < N_HEADS, _sigmoid(ba) * vmask, gco_ref[0:1, :] * softplus)

    n_sub = len(PROJ_PIECES)
    project(0)
    for r in range(n_sub):
        if r + 1 < n_sub:
            project(r + 1)
        pointwise(r)
    conv_buf[0:SUBLANES, :] = conv_buf[tm:tm + SUBLANES, :]


def _proj_call(h, lw):
    bsz, t_len, d = h.shape
    tm = PROJ_ROWS
    assert t_len % tm == 0
    full = lambda a: pl.BlockSpec(a.shape, lambda b, t: (0,) * a.ndim)
    row = lambda w: pl.BlockSpec((1, tm, w), lambda b, t: (b, t, 0))
    col = lambda w: pl.BlockSpec((1, w, tm), lambda b, t: (b, 0, t))
    weights = (lw["norm_w"], lw["w_qv_t"], lw["w_k"], lw["w_gqkv"], lw["w_ba"], lw["w_hg"], lw["sb_qn"], lw["sb_kn"],
               lw["conv_w"], lw["g_coef"])
    bw = BRANCH_WIDTH
    out_shape = (
        jax.ShapeDtypeStruct((bsz, bw, t_len), BF16),
        jax.ShapeDtypeStruct((bsz, t_len, bw), BF16),
        jax.ShapeDtypeStruct((bsz, bw, t_len), BF16),
        jax.ShapeDtypeStruct((bsz, t_len, 3 * bw), BF16),
        jax.ShapeDtypeStruct((bsz, t_len, LANES), F32),
        jax.ShapeDtypeStruct((bsz, t_len, bw), BF16),
        jax.ShapeDtypeStruct((bsz, t_len, bw), F32),
        jax.ShapeDtypeStruct((bsz, t_len, bw), BF16),
    )
    out_specs = (col(bw), row(bw), col(bw), row(3 * bw), row(LANES), row(bw), row(bw), row(bw))
    return pl.pallas_call(
        _proj_kernel,
        grid=(bsz, t_len // tm),
        in_specs=[row(d)] + [full(w) for w in weights],
        out_specs=out_specs,
        out_shape=out_shape,
        scratch_shapes=[pltpu.VMEM((tm + SUBLANES, 3 * bw), F32), pltpu.VMEM((tm, 3 * bw), F32),
                        pltpu.VMEM((2 * bw, tm), F32), pltpu.VMEM((tm, bw), F32), pltpu.VMEM((tm, LANES), F32)],
        compiler_params=_compiler_params(("arbitrary", "arbitrary")),
        name="proj",
    )(h, *weights)


def _sb_kernel(qt_ref, k_ref, vt_ref, o_ref, acc_ref, carry_ref, z_buf, attn_buf):
    t_len = k_ref.shape[1]
    n_tiles = (t_len - FRONT) // SB_TILE
    ii = lax.broadcasted_iota(jnp.int32, (SB_SUB, SB_SUB), 0)
    jj = lax.broadcasted_iota(jnp.int32, (SB_SUB, SB_SUB), 1)
    later = (jj > ii).astype(BF16)
    heads = range(SB_HEADS)
    hs = lambda hd: slice(hd * HEAD_DIM, (hd + 1) * HEAD_DIM)
    pair = [SB_TILE, SB_TILE]

    Job = collections.namedtuple("Job", "tile q0 nq k_lo widths causal front")
    starts_of = lambda job: [sum(job.widths[:i]) for i in range(len(job.widths))]

    def visit_order(jobs):
        return [(ji, bi, hd) for ji, job in enumerate(jobs)
                for bi in reversed(range(len(job.widths))) for hd in heads]

    def scores(jobs):
        z = {}
        for ji, bi, hd in visit_order(jobs):
            job = jobs[ji]
            rows = pl.ds(job.k_lo + starts_of(job)[bi], job.widths[bi])
            z[ji, bi, hd] = _dot(k_ref[0, rows, hs(hd)], qt_ref[0, hs(hd), pl.ds(job.q0, job.nq)])
        return z

    def weights(jobs, z):
        order = visit_order(jobs)
        mask = {}
        for ji, job in enumerate(jobs):
            for bi, start in enumerate(starts_of(job)):
                mask[ji, bi] = None
                inert = job.front and job.k_lo + start < PAD_FRONT
                if bi in job.causal or inert:
                    s_idx = job.k_lo + start + lax.broadcasted_iota(jnp.int32, (job.widths[bi], job.nq), 0)
                    if bi in job.causal:
                        mask[ji, bi] = s_idx < job.q0 + lax.broadcasted_iota(jnp.int32, (job.widths[bi], job.nq), 1)
                    if inert:
                        valid = s_idx >= PAD_FRONT
                        mask[ji, bi] = valid if mask[ji, bi] is None else mask[ji, bi] & valid
        log_beta, keep = {}, {}
        for key in order:
            zz = z[key]
            nz = -zz
            log_keep = jnp.minimum(nz, 0.0) - jnp.log(1.0 + jnp.exp2(jnp.minimum(zz, nz))) * LOG2E
            log_beta[key] = zz + log_keep
            if mask[key[:2]] is not None:
                log_keep = jnp.where(mask[key[:2]], log_keep, 0.0)
            keep[key] = log_keep.astype(BF16)
        passed = {}
        for ji, bi, hd in order:
            for s in reversed(range(0, jobs[ji].widths[bi], SB_SUB)):
                passed[ji, bi, hd, s] = _dot(later, keep[ji, bi, hd][s:s + SB_SUB])
        out = {}
        for ji, job in enumerate(jobs):
            for hd in heads:
                carry = carry_ref[job.tile, hd, :, :job.nq]
                parts = []
                for bi in reversed(range(len(job.widths))):
                    for s in reversed(range(0, job.widths[bi], SB_SUB)):
                        p = passed[ji, bi, hd, s]
                        attn = jnp.exp2(log_beta[ji, bi, hd][s:s + SB_SUB] + p + carry)
                        if mask[ji, bi] is not None:
                            attn = jnp.where(mask[ji, bi][s:s + SB_SUB], attn, 0.0)
                        parts.insert(0, attn.astype(BF16))
                        carry = carry + p[0:1, :] + keep[ji, bi, hd][s:s + 1, :].astype(F32)
                carry_ref[job.tile, hd, :, :job.nq] = carry
                out[ji, hd] = parts[0] if len(parts) == 1 else jnp.concatenate(parts, axis=0)
        return out

    def accumulate(jobs, attn):
        for ji, job in enumerate(jobs):
            for hd in heads:
                acc_ref[job.tile, hd, :, :job.nq] += _dot(vt_ref[0, hs(hd), pl.ds(job.k_lo, sum(job.widths))],
                                                          attn[ji, hd])

    def visit(jobs):
        accumulate(jobs, weights(jobs, scores(jobs)))

    def start():
        acc_ref[...] = jnp.zeros_like(acc_ref)
        carry_ref[...] = jnp.zeros_like(carry_ref)

    def finish(tile, q0, nq):
        for hd in heads:
            o_ref[0, pl.ds(q0, nq), hs(hd)] = acc_ref[tile, hd, :, :nq].T.astype(o_ref.dtype)

    def alive(tile):
        return (jnp.max(carry_ref[tile]) > -SB_DEAD).astype(jnp.int32)

    start()
    visit([Job(0, 0, FRONT, 0, [FRONT], [0], True)])
    finish(0, 0, FRONT)

    def walk_rest(tile, m, q0):
        rest = jnp.maximum(m - 1, 0)
        n_quads = lax.shift_right_logical(rest, 2)
        last_pair = 2 * n_quads - 1
        pair_lo = lambda i: pl.multiple_of(q0 - (2 * i + 3) * SB_TILE, LANES)
        pair_job = lambda k_lo: Job(tile, q0, SB_TILE, k_lo, pair, [], False)

        def put_scores(slot, k_lo):
            for key, zz in scores([pair_job(k_lo)]).items():
                z_buf[slot, key[1], key[2]] = zz

        def accumulate_from(slot, k_lo):
            accumulate([pair_job(k_lo)], {(0, hd): attn_buf[slot, hd] for hd in heads})

        def pair_step(slot, i):
            accumulate_from(1 - slot, pair_lo(jnp.maximum(i - 1, 0)))
            put_scores(1 - slot, pair_lo(jnp.minimum(i + 1, last_pair)))
            jobs = [pair_job(pair_lo(i))]
            z = {key: z_buf[slot, key[1], key[2]] for key in visit_order(jobs)}
            attn = weights(jobs, z)
            for hd in heads:
                attn_buf[slot, hd] = attn[0, hd]

        @pl.when(n_quads > 0)
        def _():
            attn_buf[1] = jnp.zeros_like(attn_buf[1])
            put_scores(0, pair_lo(0))

            def quad_step(state):
                j, _ = state
                pair_step(0, 2 * j)
                pair_step(1, 2 * j + 1)
                return j + 1, alive(tile)

            done, _ = lax.while_loop(lambda s: (s[0] < n_quads) & (s[1] == 1), quad_step,
                                     (jnp.int32(0), jnp.int32(1)))
            accumulate_from(1, pair_lo(2 * done - 1))

        still_alive = alive(tile)
        for left in range(min(4, n_tiles)):
            @pl.when(((rest & 3) == left) & (still_alive == 1))
            def _():
                visit([Job(tile, q0, SB_TILE, 0, [FRONT] + [SB_TILE] * left, [], True)])

    def tile_pair(first, m0):
        q0 = FRONT + m0 * SB_TILE
        q0 = q0 if first else pl.multiple_of(q0, LANES)
        q1 = q0 + SB_TILE if first else pl.multiple_of(q0 + SB_TILE, LANES)
        start()
        if first:
            visit([Job(0, q0, SB_TILE, q0, [SB_TILE], [0], False), Job(1, q1, SB_TILE, q0, pair, [1], False)])
        else:
            visit([Job(0, q0, SB_TILE, q0 - SB_TILE, pair, [1], False), Job(1, q1, SB_TILE, q0, pair, [1], False)])

        def rest_of(tile, _):
            @pl.when(alive(tile) == 1)
            def _():
                walk_rest(tile, m0 + tile, pl.multiple_of(q0 + tile * SB_TILE, LANES))
            return 0

        lax.fori_loop(0, 2, rest_of, 0)
        finish(0, q0, SB_TILE)
        finish(1, q1, SB_TILE)

    assert n_tiles % 2 == 0
    tile_pair(True, 0)

    def pair_body(p, _):
        tile_pair(False, 2 * p)
        return 0

    lax.fori_loop(1, n_tiles // 2, pair_body, 0)


def _sb_call(qt, k, vt):
    bsz, t_len, bw = k.shape
    assert (t_len - FRONT) % SB_TILE == 0 and N_HEADS % SB_HEADS == 0
    hw = SB_HEADS * HEAD_DIM
    col = pl.BlockSpec((1, hw, t_len), lambda b, h: (b, h, 0))
    row = pl.BlockSpec((1, t_len, hw), lambda b, h: (b, 0, h))
    return pl.pallas_call(
        _sb_kernel,
        grid=(bsz, N_HEADS // SB_HEADS),
        in_specs=[col, row, col],
        out_specs=row,
        out_shape=jax.ShapeDtypeStruct((bsz, t_len, bw), BF16),
        scratch_shapes=[pltpu.VMEM((2, SB_HEADS, HEAD_DIM, SB_TILE), F32), pltpu.VMEM((2, SB_HEADS, 1, SB_TILE), F32),
                        pltpu.VMEM((2, 2, SB_HEADS, SB_TILE, SB_TILE), F32),
                        pltpu.VMEM((2, SB_HEADS, 2 * SB_TILE, SB_TILE), BF16)],
        compiler_params=_compiler_params(("arbitrary", "arbitrary")),
        name="stick_breaking",
    )(qt, k, vt)


def _chunk_consts():
    ii = lax.broadcasted_iota(jnp.int32, (CHUNK, CHUNK), 0)
    jj = lax.broadcasted_iota(jnp.int32, (CHUNK, CHUNK), 1)
    return ii, jj


def _split3(x):
    hi = x.astype(BF16)
    r = x - hi.astype(F32)
    mid = r.astype(BF16)
    return hi, mid, (r - mid.astype(F32)).astype(BF16)


def _dot_01(tri, x):
    hi, mid, lo = _split3(x)
    return _dot(tri, hi) + _dot(tri, mid) + _dot(tri, lo)


def _pair_masks():
    ii = lax.broadcasted_iota(jnp.int32, (PAIR, PAIR), 0)
    jj = lax.broadcasted_iota(jnp.int32, (PAIR, PAIR), 1)
    same = (ii // CHUNK) == (jj // CHUNK)
    return same & (ii >= jj), same & (ii > jj)


def _gdn_kernel(gqkv_ref, gb_ref, on_ref, o_ref, s_ref):
    t = pl.program_id(1)
    tm = gqkv_ref.shape[1]
    bw = BRANCH_WIDTH

    @pl.when(t == 0)
    def _():
        s_ref[...] = jnp.zeros_like(s_ref)

    causal, strict = _pair_masks()
    incl = causal.astype(BF16)
    second = lax.broadcasted_iota(jnp.int32, (PAIR, 1), 0) >= CHUNK

    heads = range(N_HEADS)
    halves = range(PAIR // CHUNK)
    half = lambda c: slice(c * CHUNK, (c + 1) * CHUNK)

    def group(first, count, state):
        pairs = range(count)
        chains = [(d, hh) for d in pairs for hh in heads]
        rows = [slice((first + d) * PAIR, (first + d + 1) * PAIR) for d in pairs]
        gbc = [gb_ref[0, rows[d], :] for d in pairs]
        g_cum = [_dot_01(incl, gbc[d]) for d in pairs]
        g_cum_t = [g.T for g in g_cum]
        q = {(d, hh): gqkv_ref[0, rows[d], hh * HEAD_DIM:(hh + 1) * HEAD_DIM] for d, hh in chains}
        k = {(d, hh): gqkv_ref[0, rows[d], bw + hh * HEAD_DIM:bw + (hh + 1) * HEAD_DIM] for d, hh in chains}
        v = {(d, hh): gqkv_ref[0, rows[d], 2 * bw + hh * HEAD_DIM:2 * bw + (hh + 1) * HEAD_DIM] for d, hh in chains}
        kf = {c: k[c].astype(F32) for c in chains}
        beta = {(d, hh): gbc[d][:, hh:hh + 1] for d, hh in chains}
        gc = {(d, hh): g_cum[d][:, N_HEADS + hh:N_HEADS + hh + 1] for d, hh in chains}
        decay = {(d, hh): jnp.exp(jnp.minimum(gc[d, hh] - g_cum_t[d][N_HEADS + hh:N_HEADS + hh + 1, :], 0.0))
                 for d, hh in chains}
        kb = {c: kf[c] * beta[c] for c in chains}
        m = {c: jnp.where(strict, _dot_nt(kb[c].astype(BF16), k[c]) * decay[c], 0.0) for c in chains}
        a_qk = {c: jnp.where(causal, _dot_nt(q[c], k[c]) * decay[c], 0.0).astype(BF16) for c in chains}
        e = {c: -m[c] for c in chains}
        pb = {c: m[c].astype(BF16) for c in chains}
        for _ in range(5):
            p = {c: _dot(pb[c], pb[c]) for c in chains}
            pb = {c: p[c].astype(BF16) for c in chains}
            e = {c: e[c] + p[c] + _dot(e[c].astype(BF16), pb[c]) for c in chains}
        eg = {c: jnp.exp(gc[c]) for c in chains}
        rhs = {c: jnp.concatenate([v[c].astype(F32) * beta[c], kb[c] * eg[c]], axis=1) for c in chains}
        uwb = {c: (rhs[c] + _dot(e[c].astype(BF16), rhs[c].astype(BF16))).astype(BF16) for c in chains}
        a_uw = {c: _dot(a_qk[c], uwb[c]) for c in chains}
        q_eff = {c: (q[c].astype(F32) * eg[c] - a_uw[c][:, HEAD_DIM:]).astype(BF16) for c in chains}
        k_dec = {}
        for c in chains:
            g_last = jnp.where(second, gc[c][PAIR - 1:PAIR, :], gc[c][CHUNK - 1:CHUNK, :])
            k_dec[c] = (kf[c] * jnp.exp(g_last - gc[c])).astype(BF16)
        kuw = {(c, h2): _dot_tn(k_dec[c][half(h2)], uwb[c][half(h2)]) for c in chains for h2 in halves}
        kwb = {key: val[:, HEAD_DIM:].astype(BF16) for key, val in kuw.items()}
        for d in pairs:
            outs = [[] for _ in heads]
            for h2 in halves:
                sb = [st.astype(BF16) for st in state]
                for hh in heads:
                    outs[hh].append(a_uw[d, hh][half(h2), :HEAD_DIM] + _dot(q_eff[d, hh][half(h2)], sb[hh]))
                state = [state[hh] * jnp.exp(gc[d, hh][(h2 + 1) * CHUNK - 1:(h2 + 1) * CHUNK, :])
                         + kuw[(d, hh), h2][:, :HEAD_DIM] - _dot(kwb[(d, hh), h2], sb[hh]) for hh in heads]
            for hh in heads:
                o = jnp.concatenate(outs[hh], axis=0)
                o_ref[0, rows[d], hh * HEAD_DIM:(hh + 1) * HEAD_DIM] = _rms_rows(o, on_ref[...]).astype(o_ref.dtype)
        return state

    assert sum(GDN_GROUPS) * PAIR == tm
    state = [s_ref[hh] for hh in heads]
    first = 0
    for count in GDN_GROUPS:
        state = group(first, count, state)
        first += count
    for hh in heads:
        s_ref[hh] = state[hh]


def _gdn_call(gqkv, gb, out_norm):
    bsz, t_len, _ = gqkv.shape
    tm = PROJ_ROWS
    row = lambda w: pl.BlockSpec((1, tm, w), lambda b, t: (b, t, 0))
    return pl.pallas_call(
        _gdn_kernel,
        grid=(bsz, t_len // tm),
        in_specs=[row(3 * BRANCH_WIDTH), row(LANES), pl.BlockSpec(out_norm.shape, lambda b, t: (0, 0))],
        out_specs=row(BRANCH_WIDTH),
        out_shape=jax.ShapeDtypeStruct((bsz, t_len, BRANCH_WIDTH), BF16),
        scratch_shapes=[pltpu.VMEM((N_HEADS, HEAD_DIM, HEAD_DIM), F32)],
        compiler_params=_compiler_params(("arbitrary", "arbitrary")),
        name="gated_deltanet",
    )(gqkv, gb, out_norm)


def _hgrn_kernel(hq_ref, hf_ref, hi_ref, lb_ref, on_ref, o_ref, s_ref):
    t = pl.program_id(1)
    tm = hq_ref.shape[1]

    @pl.when(t == 0)
    def _():
        s_ref[...] = jnp.zeros_like(s_ref)

    ii, jj = _chunk_consts()
    incl = (ii >= jj).astype(BF16)
    levels = []
    half = CHUNK // 2
    while half >= HG_SUB:
        row_upper = (ii % (2 * half)) >= half
        same_group = (ii // (2 * half)) == (jj // (2 * half))
        col_lower = (jj % (2 * half)) < half
        levels.append((half, row_upper & same_group & col_lower))
        half //= 2
    sub_lane = lax.broadcasted_iota(jnp.int32, (HG_SUB, CHUNK), 1)

    heads = range(N_HEADS)
    hs = lambda hh: slice(hh * HEAD_DIM, (hh + 1) * HEAD_DIM)

    def group_body(grp, _):
        chunks = range(HG_GROUP)
        chains = [(c, hh) for c in chunks for hh in heads]
        rows = [pl.ds(pl.multiple_of((grp * HG_GROUP + c) * CHUNK, CHUNK), CHUNK) for c in chunks]
        lbh = [lb_ref[:, hs(hh)] for hh in heads]
        sig = {(c, hh): _sigmoid(hf_ref[0, rows[c], hs(hh)]) for c, hh in chains}
        k = {(c, hh): (1.0 - lbh[hh]) * (1.0 - sig[c, hh]) for c, hh in chains}
        g_cum = {(c, hh): _dot_01(incl, jnp.log(lbh[hh] + (1.0 - lbh[hh]) * sig[c, hh]) * LOG2E) for c, hh in chains}
        q = {(c, hh): hq_ref[0, rows[c], hs(hh)].astype(F32) for c, hh in chains}
        vb = {(c, hh): hi_ref[0, rows[c], hs(hh)] for c, hh in chains}
        g_end = {ch: g_cum[ch][CHUNK - 1:CHUNK, :] for ch in chains}
        k_dec = {ch: (k[ch] * jnp.exp2(g_end[ch] - g_cum[ch])).astype(BF16) for ch in chains}
        gain_t = {ch: _dot_tn(vb[ch], k_dec[ch]) for ch in chains}
        q_dec = {ch: (q[ch] * jnp.exp2(g_cum[ch])).astype(BF16) for ch in chains}

        a_off = {ch: jnp.zeros((CHUNK, CHUNK), F32) for ch in chains}
        for half, lvl_mask in levels:
            for ch in chains:
                ref_rows = jnp.concatenate(
                    [jnp.broadcast_to(g_cum[ch][g0 + half:g0 + half + 1, :], (2 * half, HEAD_DIM))
                     for g0 in range(0, CHUNK, 2 * half)], axis=0)
                q_s = q[ch] * jnp.exp2(g_cum[ch] - ref_rows)
                k_s = k[ch] * jnp.exp2(ref_rows - g_cum[ch])
                a_off[ch] = a_off[ch] + jnp.where(lvl_mask, _dot_nt(q_s.astype(BF16), k_s.astype(BF16)), 0.0)
        o_intra = {}
        for ch in chains:
            blocks = []
            for b0 in range(0, CHUNK, HG_SUB):
                g_blk = g_cum[ch][b0:b0 + HG_SUB, :]
                q_blk = q[ch][b0:b0 + HG_SUB, :]
                a_blk = jnp.zeros((HG_SUB, CHUNK), F32)
                for j in range(b0, b0 + HG_SUB):
                    dec = jnp.exp2(g_blk - g_cum[ch][j:j + 1, :])
                    col = jnp.sum(dec * (q_blk * k[ch][j:j + 1, :]), axis=-1, keepdims=True)
                    a_blk = jnp.where(sub_lane == j, col, a_blk)
                blocks.append(a_blk)
            a_diag = jnp.where(ii >= jj, jnp.concatenate(blocks, axis=0), 0.0)
            o_intra[ch] = _dot((a_off[ch] + a_diag).astype(BF16), vb[ch])

        state_t = [s_ref[hh] for hh in heads]
        for c in chunks:
            for hh in heads:
                o = o_intra[c, hh] + _dot_nt(q_dec[c, hh], state_t[hh].astype(BF16))
                o_ref[0, rows[c], hs(hh)] = _rms_rows(o, on_ref[...]).astype(o_ref.dtype)
            state_t = [state_t[hh] * jnp.exp2(g_end[c, hh]) + gain_t[c, hh] for hh in heads]
        for hh in heads:
            s_ref[hh] = state_t[hh]
        return 0

    assert (tm // CHUNK) % HG_GROUP == 0
    lax.fori_loop(0, tm // CHUNK // HG_GROUP, group_body, 0)


def _hgrn_call(hq, hf, hi, lower_bound, out_norm):
    bsz, t_len, bw = hq.shape
    tm = PROJ_ROWS
    row = pl.BlockSpec((1, tm, bw), lambda b, t: (b, t, 0))
    full = lambda a: pl.BlockSpec(a.shape, lambda b, t: (0, 0))
    return pl.pallas_call(
        _hgrn_kernel,
        grid=(bsz, t_len // tm),
        in_specs=[row, row, row, full(lower_bound), full(out_norm)],
        out_specs=row,
        out_shape=jax.ShapeDtypeStruct((bsz, t_len, bw), BF16),
        scratch_shapes=[pltpu.VMEM((N_HEADS, HEAD_DIM, HEAD_DIM), F32)],
        compiler_params=_compiler_params(("arbitrary", "arbitrary")),
        name="hgrn2",
    )(hq, hf, hi, lower_bound, out_norm)


def _merge_kernel(h_ref, osb_ref, ogd_ref, ohg_ref, nw_ref, wz_ref, wmix_ref, wb_ref, wo_ref, out_ref):
    x = h_ref[...]
    xb = _rms_rows(x, nw_ref[...]).astype(BF16)
    y = None
    for b, o_ref in enumerate((osb_ref, ogd_ref, ohg_ref)):
        zg = _dot(xb, wz_ref[:, b * BRANCH_WIDTH:(b + 1) * BRANCH_WIDTH])
        gated = (o_ref[...].astype(F32) * _silu(zg)).astype(BF16)
        mix = _sigmoid(_dot(xb, wmix_ref[:, b * D_MODEL:(b + 1) * D_MODEL]))
        term = mix * _dot(gated, wb_ref[b])
        y = term if y is None else y + term
    out_ref[...] = x + _dot(y.astype(BF16), wo_ref[...])


def _merge_call(h2, osb, ogd, ohg, lw):
    rows, d = h2.shape
    tm = MERGE_ROWS
    assert rows % tm == 0
    row = lambda w: pl.BlockSpec((tm, w), lambda i: (i, 0))
    full = lambda a: pl.BlockSpec(a.shape, lambda i: (0,) * a.ndim)
    weights = (lw["norm_w"], lw["w_z"], lw["w_mix"], lw["w_branch"], lw["w_out"])
    return pl.pallas_call(
        _merge_kernel,
        grid=(rows // tm,),
        in_specs=[row(d), row(BRANCH_WIDTH), row(BRANCH_WIDTH), row(BRANCH_WIDTH)] + [full(w) for w in weights],
        out_specs=row(d),
        out_shape=jax.ShapeDtypeStruct((rows, d), F32),
        compiler_params=_compiler_params(("arbitrary",)),
        name="merge",
    )(h2, osb, ogd, ohg, *weights)


def _layer_weights(layer, norm_w, w_in, sb_q_norm, sb_k_norm, gdn_conv_w, gdn_a_log, gdn_dt_bias, w_branch, w_out):
    bw = BRANCH_WIDTH
    w = w_in[layer]
    o_gqkv = 4 * bw
    o_gz = o_gqkv + 3 * bw
    o_gb = o_gz + bw
    o_hq = o_gb + 2 * N_HEADS
    o_hz = o_hq + 3 * bw
    o_mix = o_hz + bw
    w_ba = jnp.zeros((D_MODEL, LANES), F32).at[:, :2 * N_HEADS].set(w[:, o_gb:o_gb + 2 * N_HEADS])
    pad = LANES - 2 * N_HEADS
    neg_a = jnp.concatenate([jnp.zeros((N_HEADS,), F32), -jnp.exp(gdn_a_log[layer].astype(F32)), jnp.zeros((pad,), F32)])
    dt_b = jnp.concatenate([jnp.zeros((N_HEADS,), F32), gdn_dt_bias[layer].astype(F32), jnp.zeros((pad,), F32)])
    return {
        "norm_w": norm_w[layer].astype(F32)[None, :],
        "w_qv_t": jnp.concatenate([w[:, 0:bw], w[:, 2 * bw:3 * bw]], axis=1).T.astype(BF16),
        "w_k": w[:, bw:2 * bw].astype(BF16),
        "w_gqkv": w[:, o_gqkv:o_gqkv + 3 * bw].astype(BF16),
        "w_ba": w_ba.astype(BF16),
        "w_hg": w[:, o_hq:o_hq + 3 * bw].astype(BF16),
        "sb_qn": (sb_q_norm[layer].astype(F32) * (HEAD_DIM ** -0.5 * LOG2E))[:, None],
        "sb_kn": sb_k_norm[layer].astype(F32)[None, :],
        "conv_w": gdn_conv_w[layer].astype(F32),
        "g_coef": jnp.stack([neg_a, dt_b]),
        "w_z": jnp.concatenate([w[:, 3 * bw:4 * bw], w[:, o_gz:o_gz + bw], w[:, o_hz:o_hz + bw]], axis=1).astype(BF16),
        "w_mix": w[:, o_mix:].astype(BF16),
        "w_branch": w_branch[layer].astype(BF16),
        "w_out": w_out[layer].astype(BF16),
    }


def kernel(x, meta_tokens, norm_w, w_in, sb_q_norm, sb_k_norm, gdn_conv_w, gdn_a_log, gdn_dt_bias, gdn_out_norm,
           hgrn_lb_logits, hgrn_out_norm, w_branch, w_out):
    bsz, _, d = x.shape
    depth = w_in.shape[0]
    h = jnp.concatenate([
        jnp.zeros((bsz, PAD_FRONT, d), x.dtype),
        jnp.broadcast_to(meta_tokens.astype(x.dtype)[None], (bsz, N_META, d)),
        x], axis=1)
    t_len = h.shape[1]
    p = jax.nn.softmax(hgrn_lb_logits.astype(F32), axis=0)
    lower_bounds = jnp.cumsum(p, axis=0) - p[0:1]
    for layer in range(depth):
        lw = _layer_weights(layer, norm_w, w_in, sb_q_norm, sb_k_norm, gdn_conv_w, gdn_a_log, gdn_dt_bias,
                            w_branch, w_out)
        qt, k, vt, gqkv, gb, hq, hf, hi = _proj_call(h, lw)
        o_sb = _sb_call(qt, k, vt)
        o_gd = _gdn_call(gqkv, gb, gdn_out_norm[layer].astype(F32)[None, :])
        o_hg = _hgrn_call(hq, hf, hi, lower_bounds[layer][None, :], hgrn_out_norm[layer].astype(F32)[None, :])
        flat = lambda a: a.reshape(bsz * t_len, a.shape[-1])
        h = _merge_call(flat(h), flat(o_sb), flat(o_gd), flat(o_hg), lw).reshape(bsz, t_len, d)
    return h[:, FRONT:]
```

```python
import collections

import jax
import jax.numpy as jnp
from jax import lax
from jax.experimental import pallas as pl
from jax.experimental.pallas import tpu as pltpu

F32 = jnp.float32
BF16 = jnp.bfloat16

D_MODEL = 1024
BRANCH_WIDTH = D_MODEL // 2
HEAD_DIM = 128
N_HEADS = BRANCH_WIDTH // HEAD_DIM
N_BRANCHES = 3
CONV_WIDTH = 4
CHUNK = 64
N_META = 16
FRONT = 128
PAD_FRONT = FRONT - N_META
NORM_EPS = 1e-6

LANES = 128
SUBLANES = 8
VMEM_LIMIT_BYTES = 56 * 1024 * 1024

PROJ_ROWS = 640
PROJ_PIECES = (256, 256, 128)
MERGE_ROWS = 512
SB_TILE = 256
SB_SUB = 128
SB_HEADS = 2
SB_DEAD = 160.0
PAIR = 2 * CHUNK
GDN_GROUPS = (5,)
HG_GROUP = 5
HG_SUB = 8
LOG2E = 1.4426950408889634


def _dot(a, b):
    return jnp.dot(a, b, preferred_element_type=F32)


def _dot_nt(a, b):
    return lax.dot_general(a, b, (((1,), (1,)), ((), ())), preferred_element_type=F32)


def _dot_tn(a, b):
    return lax.dot_general(a, b, (((0,), (0,)), ((), ())), preferred_element_type=F32)


def _sigmoid(x):
    return 1.0 / (1.0 + jnp.exp(-x))


def _silu(x):
    return x * _sigmoid(x)


def _rms_rows(x, w):
    return x * lax.rsqrt(jnp.mean(x * x, axis=-1, keepdims=True) + NORM_EPS) * w


def _compiler_params(semantics):
    return pltpu.CompilerParams(dimension_semantics=semantics, vmem_limit_bytes=VMEM_LIMIT_BYTES)


def _proj_kernel(h_ref, nw_ref, wqv_ref, wk_ref, wg_ref, wba_ref, whg_ref, qn_ref, kn_ref, conv_ref, gco_ref,
                 qt_ref, k_ref, vt_ref, gqkv_ref, gb_ref, hq_ref, hf_ref, hi_ref,
                 conv_buf, hg_buf, qv_buf, k_buf, ba_buf):
    t = pl.program_id(1)
    tm = h_ref.shape[1]
    bw = BRANCH_WIDTH
    first = SUBLANES - (CONV_WIDTH - 1)

    @pl.when(t == 0)
    def _():
        conv_buf[0:SUBLANES, :] = jnp.zeros((SUBLANES, 3 * bw), F32)

    assert sum(PROJ_PIECES) == tm
    starts = [sum(PROJ_PIECES[:r]) for r in range(len(PROJ_PIECES))]

    def project(r):
        r0, n = starts[r], PROJ_PIECES[r]
        rows = slice(r0, r0 + n)
        xb = _rms_rows(h_ref[0, rows, :], nw_ref[...]).astype(BF16)
        conv_buf[SUBLANES + r0:SUBLANES + r0 + n, :] = _dot(xb, wg_ref[...])
        hg_buf[rows, :] = _dot(xb, whg_ref[...])
        qv_buf[:, rows] = _dot_nt(wqv_ref[...], xb)
        k_buf[rows, :] = _dot(xb, wk_ref[...])
        ba_buf[rows, :] = _dot(xb, wba_ref[...])

    def pointwise(r):
        r0, n = starts[r], PROJ_PIECES[r]
        rows = slice(r0, r0 + n)
        pos = t * tm + r0 + lax.broadcasted_iota(jnp.int32, (n, 1), 0)
        vmask = (pos >= PAD_FRONT).astype(F32)

        base = first + r0
        y = conv_buf[base:base + n, :] * conv_ref[0:1, :]
        for i in range(1, CONV_WIDTH):
            y = y + conv_buf[base + i:base + i + n, :] * conv_ref[i:i + 1, :]
        y = _silu(y)
        for j in range(2 * N_HEADS):
            sl = slice(j * HEAD_DIM, (j + 1) * HEAD_DIM)
            yh = y[:, sl]
            yh = yh * lax.rsqrt(jnp.sum(yh * yh, axis=-1, keepdims=True) + NORM_EPS)
            if j < N_HEADS:
                yh = yh * (HEAD_DIM ** -0.5)
            gqkv_ref[0, rows, sl] = yh.astype(BF16)
        gqkv_ref[0, rows, 2 * bw:] = y[:, 2 * bw:].astype(BF16)

        hg = hg_buf[rows, :]
        hq_ref[0, rows, :] = _silu(hg[:, :bw]).astype(BF16)
        hf_ref[0, rows, :] = hg[:, bw:2 * bw]
        hi_ref[0, rows, :] = (hg[:, 2 * bw:] * vmask).astype(BF16)

        for hh in range(N_HEADS):
            sl = slice(hh * HEAD_DIM, (hh + 1) * HEAD_DIM)
            qh = qv_buf[sl, rows]
            ms = jnp.mean(qh * qh, axis=0, keepdims=True)
            qt_ref[0, sl, rows] = (qh * lax.rsqrt(ms + NORM_EPS) * qn_ref[...]).astype(BF16)
            k_ref[0, rows, sl] = _rms_rows(k_buf[rows, sl], kn_ref[...]).astype(BF16)
        vt_ref[0, :, rows] = qv_buf[bw:, rows].astype(BF16)

        ba = ba_buf[rows, :]
        sp_in = ba + gco_ref[1:2, :]
        softplus = jnp.maximum(sp_in, 0.0) + jnp.log(1.0 + jnp.exp(-jnp.abs(sp_in)))
        lane = lax.broadcasted_iota(jnp.int32, ba.shape, 1)
        gb_ref[0, rows, :] = jnp.where(lane < N_HEADS, _sigmoid(ba) * vmask, gco_ref[0:1, :] * softplus)

    n_sub = len(PROJ_PIECES)
    project(0)
    for r in range(n_sub):
        if r + 1 < n_sub:
            project(r + 1)
        pointwise(r)
    conv_buf[0:SUBLANES, :] = conv_buf[tm:tm + SUBLANES, :]


def _proj_call(h, lw):
    bsz, t_len, d = h.shape
    tm = PROJ_ROWS
    assert t_len % tm == 0
    full = lambda a: pl.BlockSpec(a.shape, lambda b, t: (0,) * a.ndim)
    row = lambda w: pl.BlockSpec((1, tm, w), lambda b, t: (b, t, 0))
    col = lambda w: pl.BlockSpec((1, w, tm), lambda b, t: (b, 0, t))
    weights = (lw["norm_w"], lw["w_qv_t"], lw["w_k"], lw["w_gqkv"], lw["w_ba"], lw["w_hg"], lw["sb_qn"], lw["sb_kn"],
               lw["conv_w"], lw["g_coef"])
    bw = BRANCH_WIDTH
    out_shape = (
        jax.ShapeDtypeStruct((bsz, bw, t_len), BF16),
        jax.ShapeDtypeStruct((bsz, t_len, bw), BF16),
        jax.ShapeDtypeStruct((bsz, bw, t_len), BF16),
        jax.ShapeDtypeStruct((bsz, t_len, 3 * bw), BF16),
        jax.ShapeDtypeStruct((bsz, t_len, LANES), F32),
        jax.ShapeDtypeStruct((bsz, t_len, bw), BF16),
        jax.ShapeDtypeStruct((bsz, t_len, bw), F32),
        jax.ShapeDtypeStruct((bsz, t_len, bw), BF16),
    )
    out_specs = (col(bw), row(bw), col(bw), row(3 * bw), row(LANES), row(bw), row(bw), row(bw))
    return pl.pallas_call(
        _proj_kernel,
        grid=(bsz, t_len // tm),
        in_specs=[row(d)] + [full(w) for w in weights],
        out_specs=out_specs,
        out_shape=out_shape,
        scratch_shapes=[pltpu.VMEM((tm + SUBLANES, 3 * bw), F32), pltpu.VMEM((tm, 3 * bw), F32),
                        pltpu.VMEM((2 * bw, tm), F32), pltpu.VMEM((tm, bw), F32), pltpu.VMEM((tm, LANES), F32)],
        compiler_params=_compiler_params(("arbitrary", "arbitrary")),
        name="proj",
    )(h, *weights)


def _sb_kernel(qt_ref, k_ref, vt_ref, o_ref, acc_ref, carry_ref, z_buf, attn_buf):
    t_len = k_ref.shape[1]
    n_tiles = (t_len - FRONT) // SB_TILE
    ii = lax.broadcasted_iota(jnp.int32, (SB_SUB, SB_SUB), 0)
    jj = lax.broadcasted_iota(jnp.int32, (SB_SUB, SB_SUB), 1)
    later = (jj > ii).astype(BF16)
    heads = range(SB_HEADS)
    hs = lambda hd: slice(hd * HEAD_DIM, (hd + 1) * HEAD_DIM)
    pair = [SB_TILE, SB_TILE]

    Job = collections.namedtuple("Job", "tile q0 nq k_lo widths causal front")
    starts_of = lambda job: [sum(job.widths[:i]) for i in range(len(job.widths))]

    def visit_order(jobs):
        return [(ji, bi, hd) for ji, job in enumerate(jobs)
                for bi in reversed(range(len(job.widths))) for hd in heads]

    def scores(jobs):
        z = {}
        for ji, bi, hd in visit_order(jobs):
            job = jobs[ji]
            rows = pl.ds(job.k_lo + starts_of(job)[bi], job.widths[bi])
            z[ji, bi, hd] = _dot(k_ref[0, rows, hs(hd)], qt_ref[0, hs(hd), pl.ds(job.q0, job.nq)])
        return z

    def weights(jobs, z):
        order = visit_order(jobs)
        mask = {}
        for ji, job in enumerate(jobs):
            for bi, start in enumerate(starts_of(job)):
                mask[ji, bi] = None
                inert = job.front and job.k_lo + start < PAD_FRONT
                if bi in job.causal or inert:
                    s_idx = job.k_lo + start + lax.broadcasted_iota(jnp.int32, (job.widths[bi], job.nq), 0)
                    if bi in job.causal:
                        mask[ji, bi] = s_idx < job.q0 + lax.broadcasted_iota(jnp.int32, (job.widths[bi], job.nq), 1)
                    if inert:
                        valid = s_idx >= PAD_FRONT
                        mask[ji, bi] = valid if mask[ji, bi] is None else mask[ji, bi] & valid
        log_beta, keep = {}, {}
        for key in order:
            zz = z[key]
            nz = -zz
            log_keep = jnp.minimum(nz, 0.0) - jnp.log(1.0 + jnp.exp2(jnp.minimum(zz, nz))) * LOG2E
            log_beta[key] = zz + log_keep
            if mask[key[:2]] is not None:
                log_keep = jnp.where(mask[key[:2]], log_keep, 0.0)
            keep[key] = log_keep.astype(BF16)
        passed = {}
        for ji, bi, hd in order:
            for s in reversed(range(0, jobs[ji].widths[bi], SB_SUB)):
                passed[ji, bi, hd, s] = _dot(later, keep[ji, bi, hd][s:s + SB_SUB])
        out = {}
        for ji, job in enumerate(jobs):
            for hd in heads:
                carry = carry_ref[job.tile, hd, :, :job.nq]
                parts = []
                for bi in reversed(range(len(job.widths))):
                    for s in reversed(range(0, job.widths[bi], SB_SUB)):
                        p = passed[ji, bi, hd, s]
                        attn = jnp.exp2(log_beta[ji, bi, hd][s:s + SB_SUB] + p + carry)
                        if mask[ji, bi] is not None:
                            attn = jnp.where(mask[ji, bi][s:s + SB_SUB], attn, 0.0)
                        parts.insert(0, attn.astype(BF16))
                        carry = carry + p[0:1, :] + keep[ji, bi, hd][s:s + 1, :].astype(F32)
                carry_ref[job.tile, hd, :, :job.nq] = carry
                out[ji, hd] = parts[0] if len(parts) == 1 else jnp.concatenate(parts, axis=0)
        return out

    def accumulate(jobs, attn):
        for ji, job in enumerate(jobs):
            for hd in heads:
                acc_ref[job.tile, hd, :, :job.nq] += _dot(vt_ref[0, hs(hd), pl.ds(job.k_lo, sum(job.widths))],
                                                          attn[ji, hd])

    def visit(jobs):
        accumulate(jobs, weights(jobs, scores(jobs)))

    def start():
        acc_ref[...] = jnp.zeros_like(acc_ref)
        carry_ref[...] = jnp.zeros_like(carry_ref)

    def finish(tile, q0, nq):
        for hd in heads:
            o_ref[0, pl.ds(q0, nq), hs(hd)] = acc_ref[tile, hd, :, :nq].T.astype(o_ref.dtype)

    def alive(tile):
        return (jnp.max(carry_ref[tile]) > -SB_DEAD).astype(jnp.int32)

    start()
    visit([Job(0, 0, FRONT, 0, [FRONT], [0], True)])
    finish(0, 0, FRONT)

    def walk_rest(tile, m, q0):
        rest = jnp.maximum(m - 1, 0)
        n_quads = lax.shift_right_logical(rest, 2)
        last_pair = 2 * n_quads - 1
        pair_lo = lambda i: pl.multiple_of(q0 - (2 * i + 3) * SB_TILE, LANES)
        pair_job = lambda k_lo: Job(tile, q0, SB_TILE, k_lo, pair, [], False)

        def put_scores(slot, k_lo):
            for key, zz in scores([pair_job(k_lo)]).items():
                z_buf[slot, key[1], key[2]] = zz

        def accumulate_from(slot, k_lo):
            accumulate([pair_job(k_lo)], {(0, hd): attn_buf[slot, hd] for hd in heads})

        def pair_step(slot, i):
            accumulate_from(1 - slot, pair_lo(jnp.maximum(i - 1, 0)))
            put_scores(1 - slot, pair_lo(jnp.minimum(i + 1, last_pair)))
            jobs = [pair_job(pair_lo(i))]
            z = {key: z_buf[slot, key[1], key[2]] for key in visit_order(jobs)}
            attn = weights(jobs, z)
            for hd in heads:
                attn_buf[slot, hd] = attn[0, hd]

        @pl.when(n_quads > 0)
        def _():
            attn_buf[1] = jnp.zeros_like(attn_buf[1])
            put_scores(0, pair_lo(0))

            def quad_step(state):
                j, _ = state
                pair_step(0, 2 * j)
                pair_step(1, 2 * j + 1)
                return j + 1, alive(tile)

            done, _ = lax.while_loop(lambda s: (s[0] < n_quads) & (s[1] == 1), quad_step,
                                     (jnp.int32(0), jnp.int32(1)))
            accumulate_from(1, pair_lo(2 * done - 1))

        still_alive = alive(tile)
        for left in range(min(4, n_tiles)):
            @pl.when(((rest & 3) == left) & (still_alive == 1))
            def _():
                visit([Job(tile, q0, SB_TILE, 0, [FRONT] + [SB_TILE] * left, [], True)])

    def tile_pair(first, m0):
        q0 = FRONT + m0 * SB_TILE
        q0 = q0 if first else pl.multiple_of(q0, LANES)
        q1 = q0 + SB_TILE if first else pl.multiple_of(q0 + SB_TILE, LANES)
        start()
        if first:
            visit([Job(0, q0, SB_TILE, q0, [SB_TILE], [0], False), Job(1, q1, SB_TILE, q0, pair, [1], False)])
        else:
            visit([Job(0, q0, SB_TILE, q0 - SB_TILE, pair, [1], False), Job(1, q1, SB_TILE, q0, pair, [1], False)])

        def rest_of(tile, _):
            @pl.when(alive(tile) == 1)
            def _():
                walk_rest(tile, m0 + tile, pl.multiple_of(q0 + tile * SB_TILE, LANES))
            return 0

        @pl.when(jnp.max(carry_ref[...]) > -SB_DEAD)
        def _():
            lax.fori_loop(0, 2, rest_of, 0)

        finish(0, q0, SB_TILE)
        finish(1, q1, SB_TILE)

    assert n_tiles % 2 == 0
    tile_pair(True, 0)

    def pair_body(p, _):
        tile_pair(False, 2 * p)
        return 0

    lax.fori_loop(1, n_tiles // 2, pair_body, 0)


def _sb_call(qt, k, vt):
    bsz, t_len, bw = k.shape
    assert (t_len - FRONT) % SB_TILE == 0 and N_HEADS % SB_HEADS == 0
    hw = SB_HEADS * HEAD_DIM
    col = pl.BlockSpec((1, hw, t_len), lambda b, h: (b, h, 0))
    row = pl.BlockSpec((1, t_len, hw), lambda b, h: (b, 0, h))
    return pl.pallas_call(
        _sb_kernel,
        grid=(bsz, N_HEADS // SB_HEADS),
        in_specs=[col, row, col],
        out_specs=row,
        out_shape=jax.ShapeDtypeStruct((bsz, t_len, bw), BF16),
        scratch_shapes=[pltpu.VMEM((2, SB_HEADS, HEAD_DIM, SB_TILE), F32), pltpu.VMEM((2, SB_HEADS, 1, SB_TILE), F32),
                        pltpu.VMEM((2, 2, SB_HEADS, SB_TILE, SB_TILE), F32),
                        pltpu.VMEM((2, SB_HEADS, 2 * SB_TILE, SB_TILE), BF16)],
        compiler_params=_compiler_params(("arbitrary", "arbitrary")),
        name="stick_breaking",
    )(qt, k, vt)


def _chunk_consts():
    ii = lax.broadcasted_iota(jnp.int32, (CHUNK, CHUNK), 0)
    jj = lax.broadcasted_iota(jnp.int32, (CHUNK, CHUNK), 1)
    return ii, jj


def _split3(x):
    hi = x.astype(BF16)
    r = x - hi.astype(F32)
    mid = r.astype(BF16)
    return hi, mid, (r - mid.astype(F32)).astype(BF16)


def _dot_01(tri, x):
    hi, mid, lo = _split3(x)
    return _dot(tri, hi) + _dot(tri, mid) + _dot(tri, lo)


def _pair_masks():
    ii = lax.broadcasted_iota(jnp.int32, (PAIR, PAIR), 0)
    jj = lax.broadcasted_iota(jnp.int32, (PAIR, PAIR), 1)
    same = (ii // CHUNK) == (jj // CHUNK)
    return same & (ii >= jj), same & (ii > jj)


def _gdn_kernel(gqkv_ref, gb_ref, on_ref, o_ref, s_ref):
    t = pl.program_id(1)
    tm = gqkv_ref.shape[1]
    bw = BRANCH_WIDTH

    @pl.when(t == 0)
    def _():
        s_ref[...] = jnp.zeros_like(s_ref)

    causal, strict = _pair_masks()
    incl = causal.astype(BF16)
    second = lax.broadcasted_iota(jnp.int32, (PAIR, 1), 0) >= CHUNK

    heads = range(N_HEADS)
    halves = range(PAIR // CHUNK)
    half = lambda c: slice(c * CHUNK, (c + 1) * CHUNK)

    def group(first, count, state):
        pairs = range(count)
        chains = [(d, hh) for d in pairs for hh in heads]
        rows = [slice((first + d) * PAIR, (first + d + 1) * PAIR) for d in pairs]
        gbc = [gb_ref[0, rows[d], :] for d in pairs]
        g_cum = [_dot_01(incl, gbc[d]) for d in pairs]
        g_cum_t = [g.T for g in g_cum]
        q = {(d, hh): gqkv_ref[0, rows[d], hh * HEAD_DIM:(hh + 1) * HEAD_DIM] for d, hh in chains}
        k = {(d, hh): gqkv_ref[0, rows[d], bw + hh * HEAD_DIM:bw + (hh + 1) * HEAD_DIM] for d, hh in chains}
        v = {(d, hh): gqkv_ref[0, rows[d], 2 * bw + hh * HEAD_DIM:2 * bw + (hh + 1) * HEAD_DIM] for d, hh in chains}
        kf = {c: k[c].astype(F32) for c in chains}
        beta = {(d, hh): gbc[d][:, hh:hh + 1] for d, hh in chains}
        gc = {(d, hh): g_cum[d][:, N_HEADS + hh:N_HEADS + hh + 1] for d, hh in chains}
        decay = {(d, hh): jnp.exp(jnp.minimum(gc[d, hh] - g_cum_t[d][N_HEADS + hh:N_HEADS + hh + 1, :], 0.0))
                 for d, hh in chains}
        kb = {c: kf[c] * beta[c] for c in chains}
        m = {c: jnp.where(strict, _dot_nt(kb[c].astype(BF16), k[c]) * decay[c], 0.0) for c in chains}
        a_qk = {c: jnp.where(causal, _dot_nt(q[c], k[c]) * decay[c], 0.0).astype(BF16) for c in chains}
        e = {c: -m[c] for c in chains}
        pb = {c: m[c].astype(BF16) for c in chains}
        for _ in range(5):
            p = {c: _dot(pb[c], pb[c]) for c in chains}
            pb = {c: p[c].astype(BF16) for c in chains}
            e = {c: e[c] + p[c] + _dot(e[c].astype(BF16), pb[c]) for c in chains}
        eg = {c: jnp.exp(gc[c]) for c in chains}
        rhs = {c: jnp.concatenate([v[c].astype(F32) * beta[c], kb[c] * eg[c]], axis=1) for c in chains}
        uwb = {c: (rhs[c] + _dot(e[c].astype(BF16), rhs[c].astype(BF16))).astype(BF16) for c in chains}
        a_uw = {c: _dot(a_qk[c], uwb[c]) for c in chains}
        q_eff = {c: (q[c].astype(F32) * eg[c] - a_uw[c][:, HEAD_DIM:]).astype(BF16) for c in chains}
        k_dec = {}
        for c in chains:
            g_last = jnp.where(second, gc[c][PAIR - 1:PAIR, :], gc[c][CHUNK - 1:CHUNK, :])
            k_dec[c] = (kf[c] * jnp.exp(g_last - gc[c])).astype(BF16)
        kuw = {(c, h2): _dot_tn(k_dec[c][half(h2)], uwb[c][half(h2)]) for c in chains for h2 in halves}
        kwb = {key: val[:, HEAD_DIM:].astype(BF16) for key, val in kuw.items()}
        for d in pairs:
            outs = [[] for _ in heads]
            for h2 in halves:
                sb = [st.astype(BF16) for st in state]
                for hh in heads:
                    outs[hh].append(a_uw[d, hh][half(h2), :HEAD_DIM] + _dot(q_eff[d, hh][half(h2)], sb[hh]))
                state = [state[hh] * jnp.exp(gc[d, hh][(h2 + 1) * CHUNK - 1:(h2 + 1) * CHUNK, :])
                         + kuw[(d, hh), h2][:, :HEAD_DIM] - _dot(kwb[(d, hh), h2], sb[hh]) for hh in heads]
            for hh in heads:
                o = jnp.concatenate(outs[hh], axis=0)
                o_ref[0, rows[d], hh * HEAD_DIM:(hh + 1) * HEAD_DIM] = _rms_rows(o, on_ref[...]).astype(o_ref.dtype)
        return state

    assert sum(GDN_GROUPS) * PAIR == tm
    state = [s_ref[hh] for hh in heads]
    first = 0
    for count in GDN_GROUPS:
        state = group(first, count, state)
        first += count
    for hh in heads:
        s_ref[hh] = state[hh]


def _gdn_call(gqkv, gb, out_norm):
    bsz, t_len, _ = gqkv.shape
    tm = PROJ_ROWS
    row = lambda w: pl.BlockSpec((1, tm, w), lambda b, t: (b, t, 0))
    return pl.pallas_call(
        _gdn_kernel,
        grid=(bsz, t_len // tm),
        in_specs=[row(3 * BRANCH_WIDTH), row(LANES), pl.BlockSpec(out_norm.shape, lambda b, t: (0, 0))],
        out_specs=row(BRANCH_WIDTH),
        out_shape=jax.ShapeDtypeStruct((bsz, t_len, BRANCH_WIDTH), BF16),
        scratch_shapes=[pltpu.VMEM((N_HEADS, HEAD_DIM, HEAD_DIM), F32)],
        compiler_params=_compiler_params(("arbitrary", "arbitrary")),
        name="gated_deltanet",
    )(gqkv, gb, out_norm)


def _hgrn_kernel(hq_ref, hf_ref, hi_ref, lb_ref, on_ref, o_ref, s_ref, row_buf):
    t = pl.program_id(1)
    tm = hq_ref.shape[1]

    @pl.when(t == 0)
    def _():
        s_ref[...] = jnp.zeros_like(s_ref)

    ii, jj = _chunk_consts()
    incl = (ii >= jj).astype(BF16)
    levels = []
    half = CHUNK // 2
    while half >= HG_SUB:
        row_upper = (ii % (2 * half)) >= half
        same_group = (ii // (2 * half)) == (jj // (2 * half))
        col_lower = (jj % (2 * half)) < half
        levels.append((half, row_upper & same_group & col_lower))
        half //= 2
    sub_lane = lax.broadcasted_iota(jnp.int32, (HG_SUB, CHUNK), 1)

    heads = range(N_HEADS)
    hs = lambda hh: slice(hh * HEAD_DIM, (hh + 1) * HEAD_DIM)

    def group_body(grp, _):
        chunks = range(HG_GROUP)
        chains = [(c, hh) for c in chunks for hh in heads]
        rows = [pl.ds(pl.multiple_of((grp * HG_GROUP + c) * CHUNK, CHUNK), CHUNK) for c in chunks]
        lbh = [lb_ref[:, hs(hh)] for hh in heads]
        sig = {(c, hh): _sigmoid(hf_ref[0, rows[c], hs(hh)]) for c, hh in chains}
        k = {(c, hh): (1.0 - lbh[hh]) * (1.0 - sig[c, hh]) for c, hh in chains}
        g_cum = {(c, hh): _dot_01(incl, jnp.log(lbh[hh] + (1.0 - lbh[hh]) * sig[c, hh]) * LOG2E) for c, hh in chains}
        q = {(c, hh): hq_ref[0, rows[c], hs(hh)].astype(F32) for c, hh in chains}
        vb = {(c, hh): hi_ref[0, rows[c], hs(hh)] for c, hh in chains}
        g_end = {ch: g_cum[ch][CHUNK - 1:CHUNK, :] for ch in chains}
        k_dec = {ch: (k[ch] * jnp.exp2(g_end[ch] - g_cum[ch])).astype(BF16) for ch in chains}
        gain_t = {ch: _dot_tn(vb[ch], k_dec[ch]) for ch in chains}
        q_dec = {ch: (q[ch] * jnp.exp2(g_cum[ch])).astype(BF16) for ch in chains}

        a_off = {ch: jnp.zeros((CHUNK, CHUNK), F32) for ch in chains}
        for half, lvl_mask in levels:
            for ch in chains:
                ref_rows = jnp.concatenate(
                    [jnp.broadcast_to(g_cum[ch][g0 + half:g0 + half + 1, :], (2 * half, HEAD_DIM))
                     for g0 in range(0, CHUNK, 2 * half)], axis=0)
                q_s = q[ch] * jnp.exp2(g_cum[ch] - ref_rows)
                k_s = k[ch] * jnp.exp2(ref_rows - g_cum[ch])
                a_off[ch] = a_off[ch] + jnp.where(lvl_mask, _dot_nt(q_s.astype(BF16), k_s.astype(BF16)), 0.0)
        o_intra = {}
        for ci, ch in enumerate(chains):
            row_buf[0, ci] = g_cum[ch]
            row_buf[1, ci] = k[ch]
            blocks = []
            for b0 in range(0, CHUNK, HG_SUB):
                g_blk = g_cum[ch][b0:b0 + HG_SUB, :]
                q_blk = q[ch][b0:b0 + HG_SUB, :]
                a_blk = jnp.zeros((HG_SUB, CHUNK), F32)
                for j in range(b0, b0 + HG_SUB):
                    dec = jnp.exp2(g_blk - row_buf[0, ci, j:j + 1, :])
                    col = jnp.sum(dec * (q_blk * row_buf[1, ci, j:j + 1, :]), axis=-1, keepdims=True)
                    a_blk = jnp.where(sub_lane == j, col, a_blk)
                blocks.append(a_blk)
            a_diag = jnp.where(ii >= jj, jnp.concatenate(blocks, axis=0), 0.0)
            o_intra[ch] = _dot((a_off[ch] + a_diag).astype(BF16), vb[ch])

        state_t = [s_ref[hh] for hh in heads]
        for c in chunks:
            for hh in heads:
                o = o_intra[c, hh] + _dot_nt(q_dec[c, hh], state_t[hh].astype(BF16))
                o_ref[0, rows[c], hs(hh)] = _rms_rows(o, on_ref[...]).astype(o_ref.dtype)
            state_t = [state_t[hh] * jnp.exp2(g_end[c, hh]) + gain_t[c, hh] for hh in heads]
        for hh in heads:
            s_ref[hh] = state_t[hh]
        return 0

    assert (tm // CHUNK) % HG_GROUP == 0
    lax.fori_loop(0, tm // CHUNK // HG_GROUP, group_body, 0)


def _hgrn_call(hq, hf, hi, lower_bound, out_norm):
    bsz, t_len, bw = hq.shape
    tm = PROJ_ROWS
    row = pl.BlockSpec((1, tm, bw), lambda b, t: (b, t, 0))
    full = lambda a: pl.BlockSpec(a.shape, lambda b, t: (0, 0))
    return pl.pallas_call(
        _hgrn_kernel,
        grid=(bsz, t_len // tm),
        in_specs=[row, row, row, full(lower_bound), full(out_norm)],
        out_specs=row,
        out_shape=jax.ShapeDtypeStruct((bsz, t_len, bw), BF16),
        scratch_shapes=[pltpu.VMEM((N_HEADS, HEAD_DIM, HEAD_DIM), F32),
                        pltpu.VMEM((2, HG_GROUP * N_HEADS, CHUNK, HEAD_DIM), F32)],
        compiler_params=_compiler_params(("arbitrary", "arbitrary")),
        name="hgrn2",
    )(hq, hf, hi, lower_bound, out_norm)


def _merge_kernel(h_ref, osb_ref, ogd_ref, ohg_ref, nw_ref, wz_ref, wmix_ref, wb_ref, wo_ref, out_ref):
    x = h_ref[...]
    xb = _rms_rows(x, nw_ref[...]).astype(BF16)
    y = None
    for b, o_ref in enumerate((osb_ref, ogd_ref, ohg_ref)):
        zg = _dot(xb, wz_ref[:, b * BRANCH_WIDTH:(b + 1) * BRANCH_WIDTH])
        gated = (o_ref[...].astype(F32) * _silu(zg)).astype(BF16)
        mix = _sigmoid(_dot(xb, wmix_ref[:, b * D_MODEL:(b + 1) * D_MODEL]))
        term = mix * _dot(gated, wb_ref[b])
        y = term if y is None else y + term
    out_ref[...] = x + _dot(y.astype(BF16), wo_ref[...])


def _merge_call(h2, osb, ogd, ohg, lw):
    rows, d = h2.shape
    tm = MERGE_ROWS
    assert rows % tm == 0
    row = lambda w: pl.BlockSpec((tm, w), lambda i: (i, 0))
    full = lambda a: pl.BlockSpec(a.shape, lambda i: (0,) * a.ndim)
    weights = (lw["norm_w"], lw["w_z"], lw["w_mix"], lw["w_branch"], lw["w_out"])
    return pl.pallas_call(
        _merge_kernel,
        grid=(rows // tm,),
        in_specs=[row(d), row(BRANCH_WIDTH), row(BRANCH_WIDTH), row(BRANCH_WIDTH)] + [full(w) for w in weights],
        out_specs=row(d),
        out_shape=jax.ShapeDtypeStruct((rows, d), F32),
        compiler_params=_compiler_params(("arbitrary",)),
        name="merge",
    )(h2, osb, ogd, ohg, *weights)


def _layer_weights(layer, norm_w, w_in, sb_q_norm, sb_k_norm, gdn_conv_w, gdn_a_log, gdn_dt_bias, w_branch, w_out):
    bw = BRANCH_WIDTH
    w = w_in[layer]
    o_gqkv = 4 * bw
    o_gz = o_gqkv + 3 * bw
    o_gb = o_gz + bw
    o_hq = o_gb + 2 * N_HEADS
    o_hz = o_hq + 3 * bw
    o_mix = o_hz + bw
    w_ba = jnp.zeros((D_MODEL, LANES), F32).at[:, :2 * N_HEADS].set(w[:, o_gb:o_gb + 2 * N_HEADS])
    pad = LANES - 2 * N_HEADS
    neg_a = jnp.concatenate([jnp.zeros((N_HEADS,), F32), -jnp.exp(gdn_a_log[layer].astype(F32)), jnp.zeros((pad,), F32)])
    dt_b = jnp.concatenate([jnp.zeros((N_HEADS,), F32), gdn_dt_bias[layer].astype(F32), jnp.zeros((pad,), F32)])
    return {
        "norm_w": norm_w[layer].astype(F32)[None, :],
        "w_qv_t": jnp.concatenate([w[:, 0:bw], w[:, 2 * bw:3 * bw]], axis=1).T.astype(BF16),
        "w_k": w[:, bw:2 * bw].astype(BF16),
        "w_gqkv": w[:, o_gqkv:o_gqkv + 3 * bw].astype(BF16),
        "w_ba": w_ba.astype(BF16),
        "w_hg": w[:, o_hq:o_hq + 3 * bw].astype(BF16),
        "sb_qn": (sb_q_norm[layer].astype(F32) * (HEAD_DIM ** -0.5 * LOG2E))[:, None],
        "sb_kn": sb_k_norm[layer].astype(F32)[None, :],
        "conv_w": gdn_conv_w[layer].astype(F32),
        "g_coef": jnp.stack([neg_a, dt_b]),
        "w_z": jnp.concatenate([w[:, 3 * bw:4 * bw], w[:, o_gz:o_gz + bw], w[:, o_hz:o_hz + bw]], axis=1).astype(BF16),
        "w_mix": w[:, o_mix:].astype(BF16),
        "w_branch": w_branch[layer].astype(BF16),
        "w_out": w_out[layer].astype(BF16),
    }


def kernel(x, meta_tokens, norm_w, w_in, sb_q_norm, sb_k_norm, gdn_conv_w, gdn_a_log, gdn_dt_bias, gdn_out_norm,
           hgrn_lb_logits, hgrn_out_norm, w_branch, w_out):
    bsz, _, d = x.shape
    depth = w_in.shape[0]
    h = jnp.concatenate([
        jnp.zeros((bsz, PAD_FRONT, d), x.dtype),
        jnp.broadcast_to(meta_tokens.astype(x.dtype)[None], (bsz, N_META, d)),
        x], axis=1)
    t_len = h.shape[1]
    p = jax.nn.softmax(hgrn_lb_logits.astype(F32), axis=0)
    lower_bounds = jnp.cumsum(p, axis=0) - p[0:1]
    for layer in range(depth):
        lw = _layer_weights(layer, norm_w, w_in, sb_q_norm, sb_k_norm, gdn_conv_w, gdn_a_log, gdn_dt_bias,
                            w_branch, w_out)
        qt, k, vt, gqkv, gb, hq, hf, hi = _proj_call(h, lw)
        o_sb = _sb_call(qt, k, vt)
        o_gd = _gdn_call(gqkv, gb, gdn_out_norm[layer].astype(F32)[None, :])
        o_hg = _hgrn_call(hq, hf, hi, lower_bounds[layer][None, :], hgrn_out_norm[layer].astype(F32)[None, :])
        flat = lambda a: a.reshape(bsz * t_len, a.shape[-1])
        h = _merge_call(flat(h), flat(o_sb), flat(o_gd), flat(o_hg), lw).reshape(bsz, t_len, d)
    return h[:, FRONT:]
```

```python
import collections

import jax
import jax.numpy as jnp
from jax import lax
from jax.experimental import pallas as pl
from jax.experimental.pallas import tpu as pltpu

F32 = jnp.float32
BF16 = jnp.bfloat16

D_MODEL = 1024
BRANCH_WIDTH = D_MODEL // 2
HEAD_DIM = 128
N_HEADS = BRANCH_WIDTH // HEAD_DIM
N_BRANCHES = 3
CONV_WIDTH = 4
CHUNK = 64
N_META = 16
FRONT = 128
PAD_FRONT = FRONT - N_META
NORM_EPS = 1e-6

LANES = 128
SUBLANES = 8
VMEM_LIMIT_BYTES = 56 * 1024 * 1024

PROJ_ROWS = 640
PROJ_PIECES = (256, 256, 128)
MERGE_ROWS = 512
SB_TILE = 256
SB_SUB = 128
SB_HEADS = 2
SB_DEAD = 160.0
PAIR = 2 * CHUNK
GDN_GROUPS = (5,)
HG_GROUP = 5
HG_SUB = 8
LOG2E = 1.4426950408889634


def _dot(a, b):
    return jnp.dot(a, b, preferred_element_type=F32)


def _dot_nt(a, b):
    return lax.dot_general(a, b, (((1,), (1,)), ((), ())), preferred_element_type=F32)


def _dot_tn(a, b):
    return lax.dot_general(a, b, (((0,), (0,)), ((), ())), preferred_element_type=F32)


def _sigmoid(x):
    return 1.0 / (1.0 + jnp.exp(-x))


def _silu(x):
    return x * _sigmoid(x)


def _rms_rows(x, w):
    return x * lax.rsqrt(jnp.mean(x * x, axis=-1, keepdims=True) + NORM_EPS) * w


def _compiler_params(semantics):
    return pltpu.CompilerParams(dimension_semantics=semantics, vmem_limit_bytes=VMEM_LIMIT_BYTES)


def _proj_kernel(h_ref, nw_ref, wqv_ref, wk_ref, wg_ref, wba_ref, whg_ref, qn_ref, kn_ref, conv_ref, gco_ref,
                 qt_ref, k_ref, vt_ref, gqkv_ref, gb_ref, hq_ref, hf_ref, hi_ref,
                 conv_buf, hg_buf, qv_buf, k_buf, ba_buf):
    t = pl.program_id(1)
    tm = h_ref.shape[1]
    bw = BRANCH_WIDTH
    first = SUBLANES - (CONV_WIDTH - 1)

    @pl.when(t == 0)
    def _():
        conv_buf[0:SUBLANES, :] = jnp.zeros((SUBLANES, 3 * bw), F32)

    assert sum(PROJ_PIECES) == tm
    starts = [sum(PROJ_PIECES[:r]) for r in range(len(PROJ_PIECES))]

    def project(r):
        r0, n = starts[r], PROJ_PIECES[r]
        rows = slice(r0, r0 + n)
        xb = _rms_rows(h_ref[0, rows, :], nw_ref[...]).astype(BF16)
        conv_buf[SUBLANES + r0:SUBLANES + r0 + n, :] = _dot(xb, wg_ref[...])
        hg_buf[rows, :] = _dot(xb, whg_ref[...])
        qv_buf[:, rows] = _dot_nt(wqv_ref[...], xb)
        k_buf[rows, :] = _dot(xb, wk_ref[...])
        ba_buf[rows, :] = _dot(xb, wba_ref[...])

    def pointwise(r):
        r0, n = starts[r], PROJ_PIECES[r]
        rows = slice(r0, r0 + n)
        pos = t * tm + r0 + lax.broadcasted_iota(jnp.int32, (n, 1), 0)
        vmask = (pos >= PAD_FRONT).astype(F32)

        base = first + r0
        y = conv_buf[base:base + n, :] * conv_ref[0:1, :]
        for i in range(1, CONV_WIDTH):
            y = y + conv_buf[base + i:base + i + n, :] * conv_ref[i:i + 1, :]
        y = _silu(y)
        for j in range(2 * N_HEADS):
            sl = slice(j * HEAD_DIM, (j + 1) * HEAD_DIM)
            yh = y[:, sl]
            yh = yh * lax.rsqrt(jnp.sum(yh * yh, axis=-1, keepdims=True) + NORM_EPS)
            if j < N_HEADS:
                yh = yh * (HEAD_DIM ** -0.5)
            gqkv_ref[0, rows, sl] = yh.astype(BF16)
        gqkv_ref[0, rows, 2 * bw:] = y[:, 2 * bw:].astype(BF16)

        hg = hg_buf[rows, :]
        hq_ref[0, rows, :] = _silu(hg[:, :bw]).astype(BF16)
        hf_ref[0, rows, :] = hg[:, bw:2 * bw]
        hi_ref[0, rows, :] = (hg[:, 2 * bw:] * vmask).astype(BF16)

        for hh in range(N_HEADS):
            sl = slice(hh * HEAD_DIM, (hh + 1) * HEAD_DIM)
            qh = qv_buf[sl, rows]
            ms = jnp.mean(qh * qh, axis=0, keepdims=True)
            qt_ref[0, sl, rows] = (qh * lax.rsqrt(ms + NORM_EPS) * qn_ref[...]).astype(BF16)
            k_ref[0, rows, sl] = _rms_rows(k_buf[rows, sl], kn_ref[...]).astype(BF16)
        vt_ref[0, :, rows] = qv_buf[bw:, rows].astype(BF16)

        ba = ba_buf[rows, :]
        sp_in = ba + gco_ref[1:2, :]
        softplus = jnp.maximum(sp_in, 0.0) + jnp.log(1.0 + jnp.exp(-jnp.abs(sp_in)))
        lane = lax.broadcasted_iota(jnp.int32, ba.shape, 1)
        gb_ref[0, rows, :] = jnp.where(lane < N_HEADS, _sigmoid(ba) * vmask, gco_ref[0:1, :] * softplus)

    n_sub = len(PROJ_PIECES)
    project(0)
    for r in range(n_sub):
        if r + 1 < n_sub:
            project(r + 1)
        pointwise(r)
    conv_buf[0:SUBLANES, :] = conv_buf[tm:tm + SUBLANES, :]


def _proj_call(h, lw):
    bsz, t_len, d = h.shape
    tm = PROJ_ROWS
    assert t_len % tm == 0
    full = lambda a: pl.BlockSpec(a.shape, lambda b, t: (0,) * a.ndim)
    row = lambda w: pl.BlockSpec((1, tm, w), lambda b, t: (b, t, 0))
    col = lambda w: pl.BlockSpec((1, w, tm), lambda b, t: (b, 0, t))
    weights = (lw["norm_w"], lw["w_qv_t"], lw["w_k"], lw["w_gqkv"], lw["w_ba"], lw["w_hg"], lw["sb_qn"], lw["sb_kn"],
               lw["conv_w"], lw["g_coef"])
    bw = BRANCH_WIDTH
    out_shape = (
        jax.ShapeDtypeStruct((bsz, bw, t_len), BF16),
        jax.ShapeDtypeStruct((bsz, t_len, bw), BF16),
        jax.ShapeDtypeStruct((bsz, bw, t_len), BF16),
        jax.ShapeDtypeStruct((bsz, t_len, 3 * bw), BF16),
        jax.ShapeDtypeStruct((bsz, t_len, LANES), F32),
        jax.ShapeDtypeStruct((bsz, t_len, bw), BF16),
        jax.ShapeDtypeStruct((bsz, t_len, bw), F32),
        jax.ShapeDtypeStruct((bsz, t_len, bw), BF16),
    )
    out_specs = (col(bw), row(bw), col(bw), row(3 * bw), row(LANES), row(bw), row(bw), row(bw))
    return pl.pallas_call(
        _proj_kernel,
        grid=(bsz, t_len // tm),
        in_specs=[row(d)] + [full(w) for w in weights],
        out_specs=out_specs,
        out_shape=out_shape,
        scratch_shapes=[pltpu.VMEM((tm + SUBLANES, 3 * bw), F32), pltpu.VMEM((tm, 3 * bw), F32),
                        pltpu.VMEM((2 * bw, tm), F32), pltpu.VMEM((tm, bw), F32), pltpu.VMEM((tm, LANES), F32)],
        compiler_params=_compiler_params(("arbitrary", "arbitrary")),
        name="proj",
    )(h, *weights)


def _sb_kernel(qt_ref, k_ref, vt_ref, o_ref, acc_ref, carry_ref, z_buf, attn_buf):
    t_len = k_ref.shape[1]
    n_tiles = (t_len - FRONT) // SB_TILE
    ii = lax.broadcasted_iota(jnp.int32, (SB_SUB, SB_SUB), 0)
    jj = lax.broadcasted_iota(jnp.int32, (SB_SUB, SB_SUB), 1)
    later = (jj > ii).astype(BF16)
    heads = range(SB_HEADS)
    hs = lambda hd: slice(hd * HEAD_DIM, (hd + 1) * HEAD_DIM)
    pair = [SB_TILE, SB_TILE]

    Job = collections.namedtuple("Job", "tile q0 nq k_lo widths causal front")
    starts_of = lambda job: [sum(job.widths[:i]) for i in range(len(job.widths))]

    def visit_order(jobs):
        return [(ji, bi, hd) for ji, job in enumerate(jobs)
                for bi in reversed(range(len(job.widths))) for hd in heads]

    def scores(jobs):
        z = {}
        for ji, bi, hd in visit_order(jobs):
            job = jobs[ji]
            rows = pl.ds(job.k_lo + starts_of(job)[bi], job.widths[bi])
            z[ji, bi, hd] = _dot(k_ref[0, rows, hs(hd)], qt_ref[0, hs(hd), pl.ds(job.q0, job.nq)])
        return z

    def weights(jobs, z):
        order = visit_order(jobs)
        mask = {}
        for ji, job in enumerate(jobs):
            for bi, start in enumerate(starts_of(job)):
                mask[ji, bi] = None
                inert = job.front and job.k_lo + start < PAD_FRONT
                if bi in job.causal or inert:
                    s_idx = job.k_lo + start + lax.broadcasted_iota(jnp.int32, (job.widths[bi], job.nq), 0)
                    if bi in job.causal:
                        mask[ji, bi] = s_idx < job.q0 + lax.broadcasted_iota(jnp.int32, (job.widths[bi], job.nq), 1)
                    if inert:
                        valid = s_idx >= PAD_FRONT
                        mask[ji, bi] = valid if mask[ji, bi] is None else mask[ji, bi] & valid
        log_beta, keep = {}, {}
        for key in order:
            zz = z[key]
            nz = -zz
            log_keep = jnp.minimum(nz, 0.0) - jnp.log(1.0 + jnp.exp2(jnp.minimum(zz, nz))) * LOG2E
            log_beta[key] = zz + log_keep
            if mask[key[:2]] is not None:
                log_keep = jnp.where(mask[key[:2]], log_keep, 0.0)
            keep[key] = log_keep.astype(BF16)
        passed = {}
        for ji, bi, hd in order:
            for s in reversed(range(0, jobs[ji].widths[bi], SB_SUB)):
                passed[ji, bi, hd, s] = _dot(later, keep[ji, bi, hd][s:s + SB_SUB])
        out = {}
        for ji, job in enumerate(jobs):
            for hd in heads:
                carry = carry_ref[job.tile, hd, :, :job.nq]
                parts = []
                for bi in reversed(range(len(job.widths))):
                    for s in reversed(range(0, job.widths[bi], SB_SUB)):
                        p = passed[ji, bi, hd, s]
                        attn = jnp.exp2(log_beta[ji, bi, hd][s:s + SB_SUB] + p + carry)
                        if mask[ji, bi] is not None:
                            attn = jnp.where(mask[ji, bi][s:s + SB_SUB], attn, 0.0)
                        parts.insert(0, attn.astype(BF16))
                        carry = carry + p[0:1, :] + keep[ji, bi, hd][s:s + 1, :].astype(F32)
                carry_ref[job.tile, hd, :, :job.nq] = carry
                out[ji, hd] = parts[0] if len(parts) == 1 else jnp.concatenate(parts, axis=0)
        return out

    def accumulate(jobs, attn):
        for ji, job in enumerate(jobs):
            for hd in heads:
                acc_ref[job.tile, hd, :, :job.nq] += _dot(vt_ref[0, hs(hd), pl.ds(job.k_lo, sum(job.widths))],
                                                          attn[ji, hd])

    def visit(jobs):
        accumulate(jobs, weights(jobs, scores(jobs)))

    def start():
        acc_ref[...] = jnp.zeros_like(acc_ref)
        carry_ref[...] = jnp.zeros_like(carry_ref)

    def finish(tile, q0, nq):
        for hd in heads:
            o_ref[0, pl.ds(q0, nq), hs(hd)] = acc_ref[tile, hd, :, :nq].T.astype(o_ref.dtype)

    def alive(tile):
        return (jnp.max(carry_ref[tile]) > -SB_DEAD).astype(jnp.int32)

    start()
    visit([Job(0, 0, FRONT, 0, [FRONT], [0], True)])
    finish(0, 0, FRONT)

    def walk_rest(tile, m, q0):
        rest = jnp.maximum(m - 1, 0)
        n_quads = lax.shift_right_logical(rest, 2)
        last_pair = 2 * n_quads - 1
        pair_lo = lambda i: pl.multiple_of(q0 - (2 * i + 3) * SB_TILE, LANES)
        pair_job = lambda k_lo: Job(tile, q0, SB_TILE, k_lo, pair, [], False)

        def put_scores(slot, k_lo):
            for key, zz in scores([pair_job(k_lo)]).items():
                z_buf[slot, key[1], key[2]] = zz

        def accumulate_from(slot, k_lo):
            accumulate([pair_job(k_lo)], {(0, hd): attn_buf[slot, hd] for hd in heads})

        def pair_step(slot, i):
            accumulate_from(1 - slot, pair_lo(jnp.maximum(i - 1, 0)))
            put_scores(1 - slot, pair_lo(jnp.minimum(i + 1, last_pair)))
            jobs = [pair_job(pair_lo(i))]
            z = {key: z_buf[slot, key[1], key[2]] for key in visit_order(jobs)}
            attn = weights(jobs, z)
            for hd in heads:
                attn_buf[slot, hd] = attn[0, hd]

        @pl.when(n_quads > 0)
        def _():
            attn_buf[1] = jnp.zeros_like(attn_buf[1])
            put_scores(0, pair_lo(0))

            def quad_step(state):
                j, _ = state
                pair_step(0, 2 * j)
                pair_step(1, 2 * j + 1)
                return j + 1, alive(tile)

            done, _ = lax.while_loop(lambda s: (s[0] < n_quads) & (s[1] == 1), quad_step,
                                     (jnp.int32(0), jnp.int32(1)))
            accumulate_from(1, pair_lo(2 * done - 1))

        still_alive = alive(tile)
        for left in range(min(4, n_tiles)):
            @pl.when(((rest & 3) == left) & (still_alive == 1))
            def _():
                visit([Job(tile, q0, SB_TILE, 0, [FRONT] + [SB_TILE] * left, [], True)])

    def tile_pair(first, m0):
        q0 = FRONT + m0 * SB_TILE
        q0 = q0 if first else pl.multiple_of(q0, LANES)
        q1 = q0 + SB_TILE if first else pl.multiple_of(q0 + SB_TILE, LANES)
        start()
        if first:
            visit([Job(0, q0, SB_TILE, q0, [SB_TILE], [0], False), Job(1, q1, SB_TILE, q0, pair, [1], False)])
        else:
            visit([Job(0, q0, SB_TILE, q0 - SB_TILE, pair, [1], False), Job(1, q1, SB_TILE, q0, pair, [1], False)])

        def rest_of(tile, _):
            @pl.when(alive(tile) == 1)
            def _():
                walk_rest(tile, m0 + tile, pl.multiple_of(q0 + tile * SB_TILE, LANES))
            return 0

        @pl.when(jnp.max(carry_ref[...]) > -SB_DEAD)
        def _():
            lax.fori_loop(0, 2, rest_of, 0)

        finish(0, q0, SB_TILE)
        finish(1, q1, SB_TILE)

    assert n_tiles % 2 == 0
    tile_pair(True, 0)

    def pair_body(p, _):
        tile_pair(False, 2 * p)
        return 0

    lax.fori_loop(1, n_tiles // 2, pair_body, 0)


def _sb_call(qt, k, vt):
    bsz, t_len, bw = k.shape
    assert (t_len - FRONT) % SB_TILE == 0 and N_HEADS % SB_HEADS == 0
    hw = SB_HEADS * HEAD_DIM
    col = pl.BlockSpec((1, hw, t_len), lambda b, h: (b, h, 0))
    row = pl.BlockSpec((1, t_len, hw), lambda b, h: (b, 0, h))
    return pl.pallas_call(
        _sb_kernel,
        grid=(bsz, N_HEADS // SB_HEADS),
        in_specs=[col, row, col],
        out_specs=row,
        out_shape=jax.ShapeDtypeStruct((bsz, t_len, bw), BF16),
        scratch_shapes=[pltpu.VMEM((2, SB_HEADS, HEAD_DIM, SB_TILE), F32), pltpu.VMEM((2, SB_HEADS, 1, SB_TILE), F32),
                        pltpu.VMEM((2, 2, SB_HEADS, SB_TILE, SB_TILE), F32),
                        pltpu.VMEM((2, SB_HEADS, 2 * SB_TILE, SB_TILE), BF16)],
        compiler_params=_compiler_params(("arbitrary", "arbitrary")),
        name="stick_breaking",
    )(qt, k, vt)


def _chunk_consts():
    ii = lax.broadcasted_iota(jnp.int32, (CHUNK, CHUNK), 0)
    jj = lax.broadcasted_iota(jnp.int32, (CHUNK, CHUNK), 1)
    return ii, jj


def _split3(x):
    hi = x.astype(BF16)
    r = x - hi.astype(F32)
    mid = r.astype(BF16)
    return hi, mid, (r - mid.astype(F32)).astype(BF16)


def _dot_01(tri, x):
    hi, mid, lo = _split3(x)
    return _dot(tri, hi) + _dot(tri, mid) + _dot(tri, lo)


def _pair_masks():
    ii = lax.broadcasted_iota(jnp.int32, (PAIR, PAIR), 0)
    jj = lax.broadcasted_iota(jnp.int32, (PAIR, PAIR), 1)
    same = (ii // CHUNK) == (jj // CHUNK)
    return same & (ii >= jj), same & (ii > jj)


def _gdn_kernel(gqkv_ref, gb_ref, on_ref, o_ref, s_ref):
    t = pl.program_id(1)
    tm = gqkv_ref.shape[1]
    bw = BRANCH_WIDTH

    @pl.when(t == 0)
    def _():
        s_ref[...] = jnp.zeros_like(s_ref)

    causal, strict = _pair_masks()
    incl = causal.astype(BF16)
    second = lax.broadcasted_iota(jnp.int32, (PAIR, 1), 0) >= CHUNK

    heads = range(N_HEADS)
    halves = range(PAIR // CHUNK)
    half = lambda c: slice(c * CHUNK, (c + 1) * CHUNK)

    def group(first, count, state):
        pairs = range(count)
        chains = [(d, hh) for d in pairs for hh in heads]
        rows = [slice((first + d) * PAIR, (first + d + 1) * PAIR) for d in pairs]
        gbc = [gb_ref[0, rows[d], :] for d in pairs]
        g_cum = [_dot_01(incl, gbc[d]) for d in pairs]
        g_cum_t = [g.T for g in g_cum]
        q = {(d, hh): gqkv_ref[0, rows[d], hh * HEAD_DIM:(hh + 1) * HEAD_DIM] for d, hh in chains}
        k = {(d, hh): gqkv_ref[0, rows[d], bw + hh * HEAD_DIM:bw + (hh + 1) * HEAD_DIM] for d, hh in chains}
        v = {(d, hh): gqkv_ref[0, rows[d], 2 * bw + hh * HEAD_DIM:2 * bw + (hh + 1) * HEAD_DIM] for d, hh in chains}
        kf = {c: k[c].astype(F32) for c in chains}
        beta = {(d, hh): gbc[d][:, hh:hh + 1] for d, hh in chains}
        gc = {(d, hh): g_cum[d][:, N_HEADS + hh:N_HEADS + hh + 1] for d, hh in chains}
        decay = {(d, hh): jnp.exp(jnp.minimum(gc[d, hh] - g_cum_t[d][N_HEADS + hh:N_HEADS + hh + 1, :], 0.0))
                 for d, hh in chains}
        kb = {c: kf[c] * beta[c] for c in chains}
        m = {c: jnp.where(strict, _dot_nt(kb[c].astype(BF16), k[c]) * decay[c], 0.0) for c in chains}
        a_qk = {c: jnp.where(causal, _dot_nt(q[c], k[c]) * decay[c], 0.0).astype(BF16) for c in chains}
        e = {c: -m[c] for c in chains}
        pb = {c: m[c].astype(BF16) for c in chains}
        for _ in range(5):
            p = {c: _dot(pb[c], pb[c]) for c in chains}
            pb = {c: p[c].astype(BF16) for c in chains}
            e = {c: e[c] + p[c] + _dot(e[c].astype(BF16), pb[c]) for c in chains}
        eg = {c: jnp.exp(gc[c]) for c in chains}
        rhs = {c: jnp.concatenate([v[c].astype(F32) * beta[c], kb[c] * eg[c]], axis=1) for c in chains}
        uwb = {c: (rhs[c] + _dot(e[c].astype(BF16), rhs[c].astype(BF16))).astype(BF16) for c in chains}
        a_uw = {c: _dot(a_qk[c], uwb[c]) for c in chains}
        q_eff = {c: (q[c].astype(F32) * eg[c] - a_uw[c][:, HEAD_DIM:]).astype(BF16) for c in chains}
        k_dec = {}
        for c in chains:
            g_last = jnp.where(second, gc[c][PAIR - 1:PAIR, :], gc[c][CHUNK - 1:CHUNK, :])
            k_dec[c] = (kf[c] * jnp.exp(g_last - gc[c])).astype(BF16)
        kuw = {(c, h2): _dot_tn(k_dec[c][half(h2)], uwb[c][half(h2)]) for c in chains for h2 in halves}
        kwb = {key: val[:, HEAD_DIM:].astype(BF16) for key, val in kuw.items()}
        for d in pairs:
            outs = [[] for _ in heads]
            for h2 in halves:
                sb = [st.astype(BF16) for st in state]
                for hh in heads:
                    outs[hh].append(a_uw[d, hh][half(h2), :HEAD_DIM] + _dot(q_eff[d, hh][half(h2)], sb[hh]))
                state = [state[hh] * jnp.exp(gc[d, hh][(h2 + 1) * CHUNK - 1:(h2 + 1) * CHUNK, :])
                         + kuw[(d, hh), h2][:, :HEAD_DIM] - _dot(kwb[(d, hh), h2], sb[hh]) for hh in heads]
            for hh in heads:
                o = jnp.concatenate(outs[hh], axis=0)
                o_ref[0, rows[d], hh * HEAD_DIM:(hh + 1) * HEAD_DIM] = _rms_rows(o, on_ref[...]).astype(o_ref.dtype)
        return state

    assert sum(GDN_GROUPS) * PAIR == tm
    state = [s_ref[hh] for hh in heads]
    first = 0
    for count in GDN_GROUPS:
        state = group(first, count, state)
        first += count
    for hh in heads:
        s_ref[hh] = state[hh]


def _gdn_call(gqkv, gb, out_norm):
    bsz, t_len, _ = gqkv.shape
    tm = PROJ_ROWS
    row = lambda w: pl.BlockSpec((1, tm, w), lambda b, t: (b, t, 0))
    return pl.pallas_call(
        _gdn_kernel,
        grid=(bsz, t_len // tm),
        in_specs=[row(3 * BRANCH_WIDTH), row(LANES), pl.BlockSpec(out_norm.shape, lambda b, t: (0, 0))],
        out_specs=row(BRANCH_WIDTH),
        out_shape=jax.ShapeDtypeStruct((bsz, t_len, BRANCH_WIDTH), BF16),
        scratch_shapes=[pltpu.VMEM((N_HEADS, HEAD_DIM, HEAD_DIM), F32)],
        compiler_params=_compiler_params(("arbitrary", "arbitrary")),
        name="gated_deltanet",
    )(gqkv, gb, out_norm)


def _hgrn_kernel(hq_ref, hf_ref, hi_ref, lb_ref, on_ref, o_ref, s_ref, row_buf):
    t = pl.program_id(1)
    tm = hq_ref.shape[1]

    @pl.when(t == 0)
    def _():
        s_ref[...] = jnp.zeros_like(s_ref)

    ii, jj = _chunk_consts()
    incl = (ii >= jj).astype(BF16)
    levels = []
    half = CHUNK // 2
    while half >= HG_SUB:
        row_upper = (ii % (2 * half)) >= half
        same_group = (ii // (2 * half)) == (jj // (2 * half))
        col_lower = (jj % (2 * half)) < half
        levels.append((half, row_upper & same_group & col_lower))
        half //= 2
    sub_lane = lax.broadcasted_iota(jnp.int32, (HG_SUB, CHUNK), 1)

    heads = range(N_HEADS)
    hs = lambda hh: slice(hh * HEAD_DIM, (hh + 1) * HEAD_DIM)

    def group_body(grp, _):
        chunks = range(HG_GROUP)
        chains = [(c, hh) for c in chunks for hh in heads]
        rows = [pl.ds(pl.multiple_of((grp * HG_GROUP + c) * CHUNK, CHUNK), CHUNK) for c in chunks]
        lbh = [lb_ref[:, hs(hh)] for hh in heads]
        sig = {(c, hh): _sigmoid(hf_ref[0, rows[c], hs(hh)]) for c, hh in chains}
        k = {(c, hh): (1.0 - lbh[hh]) * (1.0 - sig[c, hh]) for c, hh in chains}
        g_cum = {(c, hh): _dot_01(incl, jnp.log(lbh[hh] + (1.0 - lbh[hh]) * sig[c, hh]) * LOG2E) for c, hh in chains}
        q = {(c, hh): hq_ref[0, rows[c], hs(hh)].astype(F32) for c, hh in chains}
        vb = {(c, hh): hi_ref[0, rows[c], hs(hh)] for c, hh in chains}
        g_end = {ch: g_cum[ch][CHUNK - 1:CHUNK, :] for ch in chains}
        k_dec = {ch: (k[ch] * jnp.exp2(g_end[ch] - g_cum[ch])).astype(BF16) for ch in chains}
        gain_t = {ch: _dot_tn(vb[ch], k_dec[ch]) for ch in chains}
        q_dec = {ch: (q[ch] * jnp.exp2(g_cum[ch])).astype(BF16) for ch in chains}

        a_off = {ch: jnp.zeros((CHUNK, CHUNK), F32) for ch in chains}
        for half, lvl_mask in levels:
            for ch in chains:
                ref_rows = jnp.concatenate(
                    [jnp.broadcast_to(g_cum[ch][g0 + half:g0 + half + 1, :], (2 * half, HEAD_DIM))
                     for g0 in range(0, CHUNK, 2 * half)], axis=0)
                q_s = q[ch] * jnp.exp2(g_cum[ch] - ref_rows)
                k_s = k[ch] * jnp.exp2(ref_rows - g_cum[ch])
                a_off[ch] = a_off[ch] + jnp.where(lvl_mask, _dot_nt(q_s.astype(BF16), k_s.astype(BF16)), 0.0)
        o_intra = {}
        for ci, ch in enumerate(chains):
            row_buf[0, ci] = g_cum[ch]
            row_buf[1, ci] = k[ch]
            blocks = []
            for b0 in range(0, CHUNK, HG_SUB):
                g_blk = g_cum[ch][b0:b0 + HG_SUB, :]
                q_blk = q[ch][b0:b0 + HG_SUB, :]
                a_blk = jnp.zeros((HG_SUB, CHUNK), F32)
                for j in range(b0, b0 + HG_SUB):
                    dec = jnp.exp2(g_blk - row_buf[0, ci, j:j + 1, :])
                    col = jnp.sum(dec * (q_blk * row_buf[1, ci, j:j + 1, :]), axis=-1, keepdims=True)
                    a_blk = jnp.where(sub_lane == j, col, a_blk)
                blocks.append(a_blk)
            a_diag = jnp.where(ii >= jj, jnp.concatenate(blocks, axis=0), 0.0)
            o_intra[ch] = _dot((a_off[ch] + a_diag).astype(BF16), vb[ch])

        state_t = [s_ref[hh] for hh in heads]
        for c in chunks:
            for hh in heads:
                o = o_intra[c, hh] + _dot_nt(q_dec[c, hh], state_t[hh].astype(BF16))
                o_ref[0, rows[c], hs(hh)] = _rms_rows(o, on_ref[...]).astype(o_ref.dtype)
            state_t = [state_t[hh] * jnp.exp2(g_end[c, hh]) + gain_t[c, hh] for hh in heads]
        for hh in heads:
            s_ref[hh] = state_t[hh]
        return 0

    assert (tm // CHUNK) % HG_GROUP == 0
    lax.fori_loop(0, tm // CHUNK // HG_GROUP, group_body, 0)


def _hgrn_call(hq, hf, hi, lower_bound, out_norm):
    bsz, t_len, bw = hq.shape
    tm = PROJ_ROWS
    row = pl.BlockSpec((1, tm, bw), lambda b, t: (b, t, 0))
    full = lambda a: pl.BlockSpec(a.shape, lambda b, t: (0, 0))
    return pl.pallas_call(
        _hgrn_kernel,
        grid=(bsz, t_len // tm),
        in_specs=[row, row, row, full(lower_bound), full(out_norm)],
        out_specs=row,
        out_shape=jax.ShapeDtypeStruct((bsz, t_len, bw), BF16),
        scratch_shapes=[pltpu.VMEM((N_HEADS, HEAD_DIM, HEAD_DIM), F32),
                        pltpu.VMEM((2, HG_GROUP * N_HEADS, CHUNK, HEAD_DIM), F32)],
        compiler_params=_compiler_params(("arbitrary", "arbitrary")),
        name="hgrn2",
    )(hq, hf, hi, lower_bound, out_norm)


def _merge_kernel(h_ref, osb_ref, ogd_ref, ohg_ref, nw_ref, wz_ref, wmix_ref, wb_ref, wo_ref, out_ref):
    x = h_ref[...]
    xb = _rms_rows(x, nw_ref[...]).astype(BF16)
    y = None
    for b, o_ref in enumerate((osb_ref, ogd_ref, ohg_ref)):
        zg = _dot(xb, wz_ref[:, b * BRANCH_WIDTH:(b + 1) * BRANCH_WIDTH])
        gated = (o_ref[...].astype(F32) * _silu(zg)).astype(BF16)
        mix = _sigmoid(_dot(xb, wmix_ref[:, b * D_MODEL:(b + 1) * D_MODEL]))
        term = mix * _dot(gated, wb_ref[b])
        y = term if y is None else y + term
    out_ref[...] = x + _dot(y.astype(BF16), wo_ref[...])


def _merge_call(h2, osb, ogd, ohg, lw):
    rows, d = h2.shape
    tm = MERGE_ROWS
    assert rows % tm == 0
    row = lambda w: pl.BlockSpec((tm, w), lambda i: (i, 0))
    full = lambda a: pl.BlockSpec(a.shape, lambda i: (0,) * a.ndim)
    weights = (lw["norm_w"], lw["w_z"], lw["w_mix"], lw["w_branch"], lw["w_out"])
    return pl.pallas_call(
        _merge_kernel,
        grid=(rows // tm,),
        in_specs=[row(d), row(BRANCH_WIDTH), row(BRANCH_WIDTH), row(BRANCH_WIDTH)] + [full(w) for w in weights],
        out_specs=row(d),
        out_shape=jax.ShapeDtypeStruct((rows, d), F32),
        compiler_params=_compiler_params(("arbitrary",)),
        name="merge",
    )(h2, osb, ogd, ohg, *weights)


def _layer_weights(layer, norm_w, w_in, sb_q_norm, sb_k_norm, gdn_conv_w, gdn_a_log, gdn_dt_bias, w_branch, w_out):
    bw = BRANCH_WIDTH
    w = w_in[layer]
    o_gqkv = 4 * bw
    o_gz = o_gqkv + 3 * bw
    o_gb = o_gz + bw
    o_hq = o_gb + 2 * N_HEADS
    o_hz = o_hq + 3 * bw
    o_mix = o_hz + bw
    w_ba = jnp.zeros((D_MODEL, LANES), w.dtype).at[:, :2 * N_HEADS].set(w[:, o_gb:o_gb + 2 * N_HEADS])
    pad = LANES - 2 * N_HEADS
    neg_a = jnp.concatenate([jnp.zeros((N_HEADS,), F32), -jnp.exp(gdn_a_log[layer].astype(F32)), jnp.zeros((pad,), F32)])
    dt_b = jnp.concatenate([jnp.zeros((N_HEADS,), F32), gdn_dt_bias[layer].astype(F32), jnp.zeros((pad,), F32)])
    return {
        "norm_w": norm_w[layer].astype(F32)[None, :],
        "w_qv_t": jnp.concatenate([w[:, 0:bw], w[:, 2 * bw:3 * bw]], axis=1).T.astype(BF16),
        "w_k": w[:, bw:2 * bw].astype(BF16),
        "w_gqkv": w[:, o_gqkv:o_gqkv + 3 * bw].astype(BF16),
        "w_ba": w_ba.astype(BF16),
        "w_hg": w[:, o_hq:o_hq + 3 * bw].astype(BF16),
        "sb_qn": (sb_q_norm[layer].astype(F32) * (HEAD_DIM ** -0.5 * LOG2E))[:, None],
        "sb_kn": sb_k_norm[layer].astype(F32)[None, :],
        "conv_w": gdn_conv_w[layer].astype(F32),
        "g_coef": jnp.stack([neg_a, dt_b]),
        "w_z": jnp.concatenate([w[:, 3 * bw:4 * bw], w[:, o_gz:o_gz + bw], w[:, o_hz:o_hz + bw]], axis=1).astype(BF16),
        "w_mix": w[:, o_mix:].astype(BF16),
        "w_branch": w_branch[layer].astype(BF16),
        "w_out": w_out[layer].astype(BF16),
    }


def kernel(x, meta_tokens, norm_w, w_in, sb_q_norm, sb_k_norm, gdn_conv_w, gdn_a_log, gdn_dt_bias, gdn_out_norm,
           hgrn_lb_logits, hgrn_out_norm, w_branch, w_out):
    bsz, _, d = x.shape
    depth = w_in.shape[0]
    h = jnp.concatenate([
        jnp.zeros((bsz, PAD_FRONT, d), x.dtype),
        jnp.broadcast_to(meta_tokens.astype(x.dtype)[None], (bsz, N_META, d)),
        x], axis=1)
    t_len = h.shape[1]
    p = jax.nn.softmax(hgrn_lb_logits.astype(F32), axis=0)
    lower_bounds = jnp.cumsum(p, axis=0) - p[0:1]
    w_in, w_branch, w_out = (w.astype(BF16) for w in (w_in, w_branch, w_out))
    for layer in range(depth):
        lw = _layer_weights(layer, norm_w, w_in, sb_q_norm, sb_k_norm, gdn_conv_w, gdn_a_log, gdn_dt_bias,
                            w_branch, w_out)
        qt, k, vt, gqkv, gb, hq, hf, hi = _proj_call(h, lw)
        o_sb = _sb_call(qt, k, vt)
        o_gd = _gdn_call(gqkv, gb, gdn_out_norm[layer].astype(F32)[None, :])
        o_hg = _hgrn_call(hq, hf, hi, lower_bounds[layer][None, :], hgrn_out_norm[layer].astype(F32)[None, :])
        flat = lambda a: a.reshape(bsz * t_len, a.shape[-1])
        h = _merge_call(flat(h), flat(o_sb), flat(o_gd), flat(o_hg), lw).reshape(bsz, t_len, d)
    return h[:, FRONT:]
```
